```python
import math
import jax
import jax.numpy as jnp
from jax import lax
import numpy as np

D_MODEL = 1024
BATCH = 8
SEQ = 2048
DEPTH = 2
DEC_BATCH = 32
DEC_SEQ = 4
PAST_LEN = 8192
PAGE_SIZE = 128

N_HEADS = 8
HEAD_DIM = 64
D_ATTN = N_HEADS * HEAD_DIM
MOBA_BLOCK = 256
MOBA_TOPK = 3
Q_BLOCK = 128
N_GROUPS = 8
D_SGU = 512
GROUP_DIM = D_SGU // N_GROUPS
CHUNK = 128
N_BUCKETS = 32
MAX_DISTANCE = 128
EPS = 1e-6
LN_EPS = 1e-5
SPLIT_POINTS = (D_ATTN, 2 * D_ATTN, 3 * D_ATTN, 4 * D_ATTN, 4 * D_ATTN + D_SGU, 4 * D_ATTN + 2 * D_SGU, 4 * D_ATTN + 3 * D_SGU, 4 * D_ATTN + 3 * D_SGU + D_MODEL)
D_IN_PROJ = 4 * D_ATTN + 3 * D_SGU + 2 * D_MODEL

kernel_name = 'hybrid_moba_gmlp_decode_step'


def rmsnorm(x, g):
    xf = x.astype(jnp.float32)
    y = xf * lax.rsqrt(jnp.mean(xf * xf, axis=-1, keepdims=True) + EPS)
    return (y * g.astype(jnp.float32)).astype(x.dtype)


def layernorm(x, g, b):
    xf = x.astype(jnp.float32)
    mu = jnp.mean(xf, axis=-1, keepdims=True)
    var = jnp.mean(jnp.square(xf - mu), axis=-1, keepdims=True)
    y = (xf - mu) * lax.rsqrt(var + LN_EPS)
    return (y * g.astype(jnp.float32) + b.astype(jnp.float32)).astype(x.dtype)


def rel_bucket(dist):
    max_exact = N_BUCKETS // 2
    d = jnp.maximum(dist, 0)
    df = jnp.maximum(d, 1).astype(jnp.float32)
    log_b = max_exact + (jnp.log(df / max_exact) / math.log(MAX_DISTANCE / max_exact) * (N_BUCKETS - max_exact)).astype(jnp.int32)
    return jnp.where(d < max_exact, d, jnp.minimum(log_b, N_BUCKETS - 1))


def mixer_inputs(x, norm_g, w_in, ln_g, ln_b):
    b, s = x.shape[0], x.shape[1]
    h = rmsnorm(x, norm_g)
    proj = jnp.einsum('bsd,de->bse', h, w_in)
    q, k, v, g_a, u, v_s, g_b, m_a, m_b = jnp.split(proj, SPLIT_POINTS, axis=-1)
    q = q.reshape(b, s, N_HEADS, HEAD_DIM)
    k = k.reshape(b, s, N_HEADS, HEAD_DIM)
    v = v.reshape(b, s, N_HEADS, HEAD_DIM)
    u = jax.nn.gelu(u)
    v_s = layernorm(jax.nn.gelu(v_s), ln_g, ln_b)
    return q, k, v, g_a, u, v_s, g_b, m_a, m_b


def merge_branches(x, attn, sgu, g_a, g_b, m_a, m_b, w_pa, w_pb, w_o):
    b, s = x.shape[0], x.shape[1]
    y_a = attn.reshape(b, s, D_ATTN) * jax.nn.silu(g_a)
    y_b = sgu * jax.nn.silu(g_b)
    mixed = jax.nn.sigmoid(m_a) * jnp.einsum('bse,ed->bsd', y_a, w_pa) + jax.nn.sigmoid(m_b) * jnp.einsum('bse,ed->bsd', y_b, w_pb)
    return x + jnp.einsum('bsd,de->bse', mixed, w_o)


def moba_prompt(q, k, v, rel_table):
    b, s = q.shape[0], q.shape[1]
    n_blk = -(-s // MOBA_BLOCK)
    n_qb = s // Q_BLOCK
    pad = n_blk * MOBA_BLOCK - s
    kb = jnp.pad(k, ((0, 0), (0, pad), (0, 0), (0, 0))).reshape(b, n_blk, MOBA_BLOCK, N_HEADS, HEAD_DIM).transpose(0, 3, 1, 2, 4)
    vb = jnp.pad(v, ((0, 0), (0, pad), (0, 0), (0, 0))).reshape(b, n_blk, MOBA_BLOCK, N_HEADS, HEAD_DIM).transpose(0, 3, 1, 2, 4)
    k_mean = jnp.mean(kb.astype(jnp.float32), axis=3)
    q_pos = jnp.arange(s)
    q_blk = q_pos // MOBA_BLOCK
    gate = jnp.einsum('bshd,bhnd->bhsn', q.astype(jnp.float32), k_mean)
    gate = jnp.where(jnp.arange(n_blk)[None, :] < q_blk[:, None], gate, -jnp.inf)
    n_sel = min(MOBA_TOPK, n_blk)
    _, sel = lax.top_k(gate, n_sel)
    valid = sel < q_blk[None, None, :, None]

    def to_items(a):
        return a.reshape(b, N_HEADS, n_qb, Q_BLOCK, n_sel).transpose(0, 2, 1, 3, 4).reshape(b * n_qb, N_HEADS, Q_BLOCK, n_sel)

    q_items = q.reshape(b * n_qb, Q_BLOCK, N_HEADS, HEAD_DIM)
    b_idx = jnp.repeat(jnp.arange(b), n_qb)
    qb_idx = jnp.tile(jnp.arange(n_qb), b)
    table_t = rel_table.T.astype(jnp.float32)
    offs = jnp.arange(MOBA_BLOCK)
    h_ar = jnp.arange(N_HEADS)[:, None, None]
    scale = HEAD_DIM ** -0.5
    n_k_sel = n_sel * MOBA_BLOCK

    def step(args):
        q_i, sel_i, valid_i, bi, qbi = args
        kb_i = kb[bi]
        vb_i = vb[bi]
        pos_i = qbi * Q_BLOCK + jnp.arange(Q_BLOCK)
        own = (qbi * Q_BLOCK) // MOBA_BLOCK
        k_sel = kb_i[h_ar, sel_i].astype(jnp.float32)
        v_sel = vb_i[h_ar, sel_i].astype(jnp.float32)
        k_own = lax.dynamic_index_in_dim(kb_i, own, axis=1, keepdims=False).astype(jnp.float32)
        v_own = lax.dynamic_index_in_dim(vb_i, own, axis=1, keepdims=False).astype(jnp.float32)
        qf = q_i.astype(jnp.float32) * scale
        sel_pos = sel_i[..., None] * MOBA_BLOCK + offs
        own_pos = own * MOBA_BLOCK + offs
        s_sel = jnp.einsum('qhd,hqnkd->hqnk', qf, k_sel) + table_t[h_ar[..., None], rel_bucket(pos_i[None, :, None, None] - sel_pos)]
        s_sel = jnp.where(valid_i[..., None], s_sel, -jnp.inf).reshape(N_HEADS, Q_BLOCK, n_k_sel)
        s_own = jnp.einsum('qhd,hkd->hqk', qf, k_own) + table_t[:, rel_bucket(pos_i[:, None] - own_pos[None, :])]
        s_own = jnp.where(own_pos[None, None, :] <= pos_i[None, :, None], s_own, -jnp.inf)
        p = jax.nn.softmax(jnp.concatenate([s_sel, s_own], axis=-1), axis=-1)
        out = jnp.einsum('hqk,hqkd->qhd', p[..., :n_k_sel], v_sel.reshape(N_HEADS, Q_BLOCK, n_k_sel, HEAD_DIM)) + jnp.einsum('hqk,hkd->qhd', p[..., n_k_sel:], v_own)
        return out.astype(q.dtype)

    out = lax.map(step, (q_items, to_items(sel), to_items(valid), b_idx, qb_idx))
    return out.reshape(b, s, N_HEADS, HEAD_DIM)


def moba_sample(q, k, v, cache_k, cache_v, layer, page_table, rel_table):
    db, t = q.shape[0], q.shape[1]
    n_pages = page_table.shape[1]
    past = n_pages * PAGE_SIZE
    ppb = MOBA_BLOCK // PAGE_SIZE
    own = past // MOBA_BLOCK
    n_full = own
    r = past - own * MOBA_BLOCK
    q_pos = past + jnp.arange(t)
    table_t = rel_table.T.astype(jnp.float32)
    scale = HEAD_DIM ** -0.5
    qf = q.astype(jnp.float32) * scale
    offs = jnp.arange(MOBA_BLOCK)
    logit_parts = []
    k_loc, v_loc = k, v
    if r > 0:
        own_pages = page_table[:, own * ppb:]
        k_loc = jnp.concatenate([cache_k[layer, own_pages].reshape(db, r, N_HEADS, HEAD_DIM).astype(k.dtype), k], axis=1)
        v_loc = jnp.concatenate([cache_v[layer, own_pages].reshape(db, r, N_HEADS, HEAD_DIM).astype(v.dtype), v], axis=1)
    loc_pos = own * MOBA_BLOCK + jnp.arange(r + t)
    s_loc = jnp.einsum('bthd,bkhd->bhtk', qf, k_loc.astype(jnp.float32)) + table_t[:, rel_bucket(q_pos[:, None] - loc_pos[None, :])][None]
    s_loc = jnp.where(loc_pos[None, None, None, :] <= q_pos[None, None, :, None], s_loc, -jnp.inf)
    if n_full > 0:
        k_full = cache_k[layer, page_table[:, :n_full * ppb]]
        k_mean = jnp.mean(k_full.astype(jnp.float32).reshape(db, n_full, ppb * PAGE_SIZE, N_HEADS, HEAD_DIM), axis=2)
        gate = jnp.einsum('bthd,bnhd->bhtn', q.astype(jnp.float32), k_mean)
        n_sel = min(MOBA_TOPK, n_full)
        _, sel = lax.top_k(gate, n_sel)
        page_idx = sel[..., None] * ppb + jnp.arange(ppb)
        phys = page_table[jnp.arange(db)[:, None, None, None, None], page_idx]
        h6 = jnp.arange(N_HEADS)[None, :, None, None, None, None]
        rows = jnp.arange(PAGE_SIZE)
        n_k_sel = n_sel * MOBA_BLOCK
        k_sel = cache_k[layer, phys[..., None], rows, h6].reshape(db, N_HEADS, t, n_k_sel, HEAD_DIM).astype(jnp.float32)
        v_sel = cache_v[layer, phys[..., None], rows, h6].reshape(db, N_HEADS, t, n_k_sel, HEAD_DIM).astype(jnp.float32)
        sel_pos = (sel[..., None] * MOBA_BLOCK + offs).reshape(db, N_HEADS, t, n_k_sel)
        h4 = jnp.arange(N_HEADS)[None, :, None, None]
        s_sel = jnp.einsum('bthd,bhtkd->bhtk', qf, k_sel) + table_t[h4, rel_bucket(q_pos[None, None, :, None] - sel_pos)]
        p = jax.nn.softmax(jnp.concatenate([s_sel, s_loc], axis=-1), axis=-1)
        out = jnp.einsum('bhtk,bhtkd->bthd', p[..., :n_k_sel], v_sel) + jnp.einsum('bhtk,bkhd->bthd', p[..., n_k_sel:], v_loc.astype(jnp.float32))
    else:
        p = jax.nn.softmax(s_loc, axis=-1)
        out = jnp.einsum('bhtk,bkhd->bthd', p, v_loc.astype(jnp.float32))
    return out.astype(q.dtype)


def sgu_prompt(u, v_n, w_s, b_s):
    b, s = u.shape[0], u.shape[1]
    nc = s // CHUNK
    vg = v_n.reshape(b, nc, CHUNK, N_GROUPS, GROUP_DIM)
    w = jnp.tril(w_s)
    z = jnp.einsum('gts,bcsgd->bctgd', w, vg) + b_s.T[None, None, :, :, None]
    return u * z.reshape(b, s, D_SGU)


def sgu_sample(u, v_n, w_s, b_s):
    b, t = u.shape[0], u.shape[1]
    vg = v_n.reshape(b, t, N_GROUPS, GROUP_DIM)
    w = jnp.tril(w_s[:, :t, :t])
    z = jnp.einsum('gts,bsgd->btgd', w, vg) + b_s[:, :t].T[None, :, :, None]
    return u * z.reshape(b, t, D_SGU)


def setup_inputs(seed: int = 0) -> dict:
    key = jax.random.key(seed)
    ks = jax.random.split(key, 18)
    n_pages = PAST_LEN // PAGE_SIZE
    n_used = DEC_BATCH * n_pages
    n_phys = (5 * n_used) // 4
    nrm = jax.random.normal
    f32 = jnp.float32
    x_prompt = nrm(ks[0], (BATCH, SEQ, D_MODEL), f32)
    x_sample = nrm(ks[1], (DEC_BATCH, DEC_SEQ, D_MODEL), f32)
    cache_k = nrm(ks[2], (DEPTH, n_phys, PAGE_SIZE, N_HEADS, HEAD_DIM), f32)
    cache_v = nrm(ks[3], (DEPTH, n_phys, PAGE_SIZE, N_HEADS, HEAD_DIM), f32)
    page_table = jax.random.permutation(ks[4], n_phys)[:n_used].reshape(DEC_BATCH, n_pages).astype(jnp.int32)
    rel_bias = 0.5 * nrm(ks[5], (N_BUCKETS, N_HEADS), f32)
    norm_g = 1.0 + 0.1 * nrm(ks[6], (DEPTH, D_MODEL), f32)
    w_in = nrm(ks[7], (DEPTH, D_MODEL, D_IN_PROJ), f32) * D_MODEL ** -0.5
    sgu_ln_g = 1.0 + 0.1 * nrm(ks[8], (DEPTH, D_SGU), f32)
    sgu_ln_b = 0.02 * nrm(ks[9], (DEPTH, D_SGU), f32)
    w_s = nrm(ks[10], (DEPTH, N_GROUPS, CHUNK, CHUNK), f32) * CHUNK ** -0.5
    b_s = 1.0 + 0.1 * nrm(ks[11], (DEPTH, N_GROUPS, CHUNK), f32)
    w_pa = nrm(ks[12], (DEPTH, D_ATTN, D_MODEL), f32) * D_ATTN ** -0.5
    w_pb = nrm(ks[13], (DEPTH, D_SGU, D_MODEL), f32) * D_SGU ** -0.5
    w_o = nrm(ks[14], (DEPTH, D_MODEL, D_MODEL), f32) * D_MODEL ** -0.5
    final_norm_g = 1.0 + 0.1 * nrm(ks[15], (D_MODEL,), f32)
    return {'x_prompt': x_prompt, 'x_sample': x_sample, 'cache_k': cache_k, 'cache_v': cache_v,
            'page_table': page_table, 'rel_bias': rel_bias, 'norm_g': norm_g, 'w_in': w_in,
            'sgu_ln_g': sgu_ln_g, 'sgu_ln_b': sgu_ln_b, 'w_s': w_s, 'b_s': b_s,
            'w_pa': w_pa, 'w_pb': w_pb, 'w_o': w_o, 'final_norm_g': final_norm_g}


def reference(x_prompt, x_sample, cache_k, cache_v, page_table, rel_bias, norm_g, w_in, sgu_ln_g, sgu_ln_b, w_s, b_s, w_pa, w_pb, w_o, final_norm_g):
    xp, xs = x_prompt, x_sample
    k_p_rows, v_p_rows, k_s_rows, v_s_rows, sgu_rows = [], [], [], [], []
    for l in range(DEPTH):
        q, k, v, g_a, u, v_n, g_b, m_a, m_b = mixer_inputs(xp, norm_g[l], w_in[l], sgu_ln_g[l], sgu_ln_b[l])
        attn = moba_prompt(q, k, v, rel_bias)
        sgu = sgu_prompt(u, v_n, w_s[l], b_s[l])
        xp = merge_branches(xp, attn, sgu, g_a, g_b, m_a, m_b, w_pa[l], w_pb[l], w_o[l])
        k_p_rows.append(k)
        v_p_rows.append(v)
        q, k, v, g_a, u, v_n, g_b, m_a, m_b = mixer_inputs(xs, norm_g[l], w_in[l], sgu_ln_g[l], sgu_ln_b[l])
        attn = moba_sample(q, k, v, cache_k, cache_v, l, page_table, rel_bias)
        sgu = sgu_sample(u, v_n, w_s[l], b_s[l])
        xs = merge_branches(xs, attn, sgu, g_a, g_b, m_a, m_b, w_pa[l], w_pb[l], w_o[l])
        k_s_rows.append(k)
        v_s_rows.append(v)
        sgu_rows.append(v_n)
    y_prompt = rmsnorm(xp, final_norm_g)
    y_sample = rmsnorm(xs, final_norm_g)
    k_prompt_new = jnp.stack(k_p_rows)
    v_prompt_new = jnp.stack(v_p_rows)
    k_sample_new = jnp.stack(k_s_rows)
    v_sample_new = jnp.stack(v_s_rows)
    sgu_v_sample_new = jnp.stack(sgu_rows)
    return (y_prompt, y_sample, k_prompt_new, v_prompt_new, k_sample_new, v_sample_new, sgu_v_sample_new)
```

```python
import functools
import math

import numpy as np
import jax
import jax.numpy as jnp
from jax import lax
from jax.experimental import pallas as pl
from jax.experimental.pallas import tpu as pltpu

N_HEADS = 8
HEAD_DIM = 64
D_ATTN = N_HEADS * HEAD_DIM
MOBA_BLOCK = 256
MOBA_TOPK = 3
N_GROUPS = 8
CHUNK = 128
N_BUCKETS = 32
MAX_DISTANCE = 128
EPS = 1e-6
LN_EPS = 1e-5
SCALE = HEAD_DIM ** -0.5

F32 = jnp.float32
BF16 = jnp.bfloat16
NEG_INF = float("-inf")

VMEM_LIMIT_BYTES = 56 * 1024 * 1024


def _bucket_thresholds():
    max_exact = N_BUCKETS // 2
    d = np.arange(max_exact, 4 * MAX_DISTANCE, dtype=np.float64)
    log_b = max_exact + (np.log(d / max_exact) / math.log(MAX_DISTANCE / max_exact) * (N_BUCKETS - max_exact)).astype(np.int64)
    bucket = np.minimum(log_b, N_BUCKETS - 1)
    return [int(d[np.argmax(bucket >= b)]) for b in range(max_exact + 1, N_BUCKETS)]


BUCKET_THRESHOLDS = _bucket_thresholds()
FAR_DISTANCE = BUCKET_THRESHOLDS[-1]


def _rel_bucket(dist):
    max_exact = N_BUCKETS // 2
    log_b = jnp.full(dist.shape, max_exact, jnp.int32)
    for thr in BUCKET_THRESHOLDS:
        log_b = log_b + jnp.where(dist >= thr, 1, 0)
    return jnp.where(dist < max_exact, dist, log_b)


def _gelu_tanh(x):
    return 0.5 * x * (1.0 + jnp.tanh(math.sqrt(2.0 / math.pi) * (x + 0.044715 * (x * x * x))))


def _silu(x):
    return x * jax.nn.sigmoid(x)


def _prompt_bias_kernel(tab_ref, o_ref):
    h = pl.program_id(0)
    c = pl.program_id(1)
    shape = o_ref.shape
    key = lax.broadcasted_iota(jnp.int32, shape, 0)
    qry = lax.broadcasted_iota(jnp.int32, shape, 1)
    dist = c * MOBA_BLOCK + qry - key
    bucket = _rel_bucket(dist)
    val = jnp.full(shape, tab_ref[N_BUCKETS - 1, h], F32)
    for b in range(N_BUCKETS - 1):
        val = jnp.where(bucket == b, tab_ref[b, h], val)
    o_ref[...] = jnp.where(dist < 0, NEG_INF, val)


def _prompt_bias_tiles(rel_bias):
    return pl.pallas_call(
        _prompt_bias_kernel,
        grid=(N_HEADS, 3),
        in_specs=[pl.BlockSpec(memory_space=pltpu.SMEM)],
        out_specs=pl.BlockSpec((None, None, MOBA_BLOCK, MOBA_BLOCK), lambda h, c: (h, c, 0, 0)),
        out_shape=jax.ShapeDtypeStruct((N_HEADS, 3, MOBA_BLOCK, MOBA_BLOCK), F32),
        compiler_params=pltpu.CompilerParams(dimension_semantics=("arbitrary", "arbitrary")),
        name="prompt_bias_tiles",
    )(rel_bias)


def _sample_bias_kernel(tab_ref, pages_ref, loc_ref, *, t_new, page):
    tab = tab_ref[...]

    def lookup(dist):
        bucket = _rel_bucket(dist)
        val = jnp.broadcast_to(tab[:, N_BUCKETS - 1:N_BUCKETS], dist.shape)
        for b in range(N_BUCKETS - 1):
            val = jnp.where(bucket == b, tab[:, b:b + 1], val)
        return jnp.where(dist < 0, NEG_INF, val)

    shape = pages_ref.shape[1:]
    row = lax.broadcasted_iota(jnp.int32, shape, 0)
    lane = lax.broadcasted_iota(jnp.int32, shape, 1)
    own = (lane % N_HEADS) == (row // t_new)
    key = lane // N_HEADS
    t = row % t_new
    for c in range(pages_ref.shape[0]):
        dist = jnp.full(shape, FAR_DISTANCE, jnp.int32) if c == 0 else MOBA_BLOCK - (c - 1) * page + t - key
        pages_ref[c] = jnp.where(own, lookup(dist), NEG_INF)
    row = lax.broadcasted_iota(jnp.int32, loc_ref.shape, 0)
    lane = lax.broadcasted_iota(jnp.int32, loc_ref.shape, 1)
    loc_ref[...] = lookup(row % t_new - lane)


def _sample_bias(rel_bias, t_new, page):
    rows = N_HEADS * t_new
    tab_rows = jnp.repeat(rel_bias.T, t_new, axis=0)
    return pl.pallas_call(
        functools.partial(_sample_bias_kernel, t_new=t_new, page=page),
        out_shape=[jax.ShapeDtypeStruct((1 + MOBA_BLOCK // page, rows, page * N_HEADS), F32),
                   jax.ShapeDtypeStruct((rows, 128), F32)],
        name="sample_bias",
    )(tab_rows)


def _inproj_kernel(x_ref, g_ref, w_ref, lng_ref, lnb_ref,
                   q_ref, k_ref, v_ref, ga_ref, u_ref, vn_ref, gb_ref, ma_ref, mb_ref):
    x = x_ref[...]
    d_model = x.shape[-1]
    d_sgu = u_ref.shape[-1]
    h = (x * lax.rsqrt(jnp.mean(x * x, axis=-1, keepdims=True) + EPS) * g_ref[...]).astype(BF16)

    def seg(lo, width):
        return jnp.dot(h, w_ref[:, lo:lo + width], preferred_element_type=F32)

    q_ref[...] = seg(0, D_ATTN)
    k_ref[...] = seg(D_ATTN, D_ATTN)
    v_ref[...] = seg(2 * D_ATTN, D_ATTN)
    ga_ref[...] = seg(3 * D_ATTN, D_ATTN)
    base = 4 * D_ATTN
    u_ref[...] = _gelu_tanh(seg(base, d_sgu))
    vs = _gelu_tanh(seg(base + d_sgu, d_sgu))
    mu = jnp.mean(vs, axis=-1, keepdims=True)
    var = jnp.mean(jnp.square(vs - mu), axis=-1, keepdims=True)
    vn_ref[...] = (vs - mu) * lax.rsqrt(var + LN_EPS) * lng_ref[...] + lnb_ref[...]
    gb_ref[...] = seg(base + 2 * d_sgu, d_sgu)
    ma_ref[...] = seg(base + 3 * d_sgu, d_model)
    mb_ref[...] = seg(base + 3 * d_sgu + d_model, d_model)


def _inproj(x, norm_g, w_bf, ln_g, ln_b, tm):
    n_tok, d_model = x.shape
    d_sgu = ln_g.shape[-1]
    assert n_tok % tm == 0
    widths = [D_ATTN] * 4 + [d_sgu] * 3 + [d_model] * 2
    assert sum(widths) == w_bf.shape[1]
    row = lambda i: (i, 0)
    fixed = lambda i: (0, 0)
    return pl.pallas_call(
        _inproj_kernel,
        grid=(n_tok // tm,),
        in_specs=[pl.BlockSpec((tm, d_model), row),
                  pl.BlockSpec((1, d_model), fixed),
                  pl.BlockSpec(w_bf.shape, fixed),
                  pl.BlockSpec((1, d_sgu), fixed),
                  pl.BlockSpec((1, d_sgu), fixed)],
        out_specs=[pl.BlockSpec((tm, w), row) for w in widths],
        out_shape=[jax.ShapeDtypeStruct((n_tok, w), F32) for w in widths],
        compiler_params=pltpu.CompilerParams(dimension_semantics=("arbitrary",),
                                             vmem_limit_bytes=VMEM_LIMIT_BYTES),
        name="inproj",
    )(x, norm_g.reshape(1, d_model), w_bf, ln_g.reshape(1, d_sgu), ln_b.reshape(1, d_sgu))


def _moba_prompt_kernel(q_ref, k_ref, v_ref, bias_ref, o_ref, kb_scr, vt_scr, kmbd_scr, neg_scr, ot_scr):
    qt = pl.program_id(1)
    n_blk = kb_scr.shape[0]
    tq = q_ref.shape[0]

    @pl.when(qt == 0)
    def _():
        lane_head = lax.broadcasted_iota(jnp.int32, (1, D_ATTN), 1) // HEAD_DIM
        for n in range(n_blk):
            kblk = k_ref[n * MOBA_BLOCK:(n + 1) * MOBA_BLOCK, :]
            kb_scr[n] = kblk.astype(BF16)
            vt_scr[n] = v_ref[n * MOBA_BLOCK:(n + 1) * MOBA_BLOCK, :].T.astype(BF16)
            k_mean = jnp.sum(kblk, axis=0, keepdims=True) * (1.0 / MOBA_BLOCK)
            for h in range(N_HEADS):
                kmbd_scr[h * n_blk + n:h * n_blk + n + 1, :] = jnp.where(lane_head == h, k_mean, 0.0)

    q_t = (q_ref[...] * SCALE).T.astype(BF16)

    gate = jnp.dot(kmbd_scr[...].astype(BF16), q_t, preferred_element_type=F32)
    blk = lax.broadcasted_iota(jnp.int32, (n_blk, tq), 0)
    for h in range(N_HEADS):
        g = gate[h * n_blk:(h + 1) * n_blk, :]
        beaten = jnp.zeros((n_blk, tq), jnp.int32)
        for m in range(n_blk):
            gm = g[m:m + 1, :]
            wins = (gm > g) | ((gm == g) & (m < blk))
            beaten = beaten + jnp.where(wins, 1, 0) * (m < qt).astype(jnp.int32)
        selected = ((blk < qt) & (beaten < MOBA_TOPK)) | (blk == qt)
        neg_scr[h] = jnp.where(selected, 0.0, NEG_INF)

    half = lax.broadcasted_iota(jnp.int32, (2 * HEAD_DIM, tq), 0) // HEAD_DIM
    for h in range(N_HEADS):
        pair = h // 2
        q_pair = q_t[pair * 2 * HEAD_DIM:(pair + 1) * 2 * HEAD_DIM, :]
        q_head = jnp.where(half == h % 2, q_pair, jnp.zeros_like(q_pair))

        def body(i, carry, h=h, pair=pair, q_head=q_head):
            m_run, l_run, acc = carry
            n = qt - i
            k_pair = kb_scr[n, :, pair * 2 * HEAD_DIM:(pair + 1) * 2 * HEAD_DIM]
            s = jnp.dot(k_pair, q_head, preferred_element_type=F32)
            s = s + bias_ref[h, jnp.minimum(i, 2)] + neg_scr[h, pl.ds(n, 1), :]
            m_new = jnp.maximum(m_run, jnp.max(s, axis=0, keepdims=True))
            alpha = jnp.exp(m_run - m_new)
            p = jnp.exp(s - m_new)
            l_new = alpha * l_run + jnp.sum(p, axis=0, keepdims=True)
            v_t = vt_scr[n, h * HEAD_DIM:(h + 1) * HEAD_DIM, :]
            acc_new = alpha * acc + jnp.dot(v_t, p.astype(BF16), preferred_element_type=F32)
            return m_new, l_new, acc_new

        init = (jnp.full((1, tq), NEG_INF, F32), jnp.zeros((1, tq), F32), jnp.zeros((HEAD_DIM, tq), F32))
        _, l_fin, acc_fin = lax.fori_loop(0, qt + 1, body, init)
        ot_scr[h * HEAD_DIM:(h + 1) * HEAD_DIM, :] = acc_fin / l_fin

    o_ref[...] = ot_scr[...].T


def _moba_prompt(q, k, v, bias_tiles):
    b, s, _ = q.shape
    assert s % MOBA_BLOCK == 0
    n_blk = s // MOBA_BLOCK
    tq = MOBA_BLOCK
    return pl.pallas_call(
        _moba_prompt_kernel,
        grid=(b, s // tq),
        in_specs=[pl.BlockSpec((None, tq, D_ATTN), lambda i, j: (i, j, 0)),
                  pl.BlockSpec((None, s, D_ATTN), lambda i, j: (i, 0, 0)),
                  pl.BlockSpec((None, s, D_ATTN), lambda i, j: (i, 0, 0)),
                  pl.BlockSpec(bias_tiles.shape, lambda i, j: (0, 0, 0, 0))],
        out_specs=pl.BlockSpec((None, tq, D_ATTN), lambda i, j: (i, j, 0)),
        out_shape=jax.ShapeDtypeStruct((b, s, D_ATTN), F32),
        scratch_shapes=[pltpu.VMEM((n_blk, MOBA_BLOCK, D_ATTN), BF16),
                        pltpu.VMEM((n_blk, D_ATTN, MOBA_BLOCK), BF16),
                        pltpu.VMEM((N_HEADS * n_blk, D_ATTN), F32),
                        pltpu.VMEM((N_HEADS, n_blk, tq), F32),
                        pltpu.VMEM((D_ATTN, tq), F32)],
        compiler_params=pltpu.CompilerParams(dimension_semantics=("arbitrary", "arbitrary"),
                                             vmem_limit_bytes=VMEM_LIMIT_BYTES),
        name="moba_prompt",
    )(q, k, v, bias_tiles)


def _moba_sample_kernel(pt_ref, q_ref, kn_ref, vn_ref, sbp_ref, sbl_ref, *refs,
                        blocks_per_step, pages_per_block, n_blk):
    n_pages = blocks_per_step * pages_per_block
    k_pages = refs[:n_pages]
    v_pages = refs[n_pages:2 * n_pages]
    o_ref = refs[2 * n_pages]
    g_scr, m_scr, l_scr, o_scr = refs[2 * n_pages + 1:]
    del pt_ref
    step = pl.program_id(1)
    t_new = kn_ref.shape[0]
    rows = q_ref.shape[0]
    page_rows = k_pages[0].shape[0] * N_HEADS

    q_rows = q_ref[...] * SCALE
    q_bf = q_rows.astype(BF16)
    lane = lax.broadcasted_iota(jnp.int32, (rows, 128), 1)

    @pl.when(step == 0)
    def _():
        g_scr[...] = jnp.full(g_scr.shape, NEG_INF, F32)
        m_scr[...] = jnp.full(m_scr.shape, NEG_INF, F32)
        l_scr[...] = jnp.zeros(l_scr.shape, F32)

    own = sbp_ref[0] > NEG_INF
    for jb in range(blocks_per_step):
        n = step * blocks_per_step + jb
        newest = (n == n_blk - 1).astype(jnp.int32)
        gate = jnp.zeros((rows, 1), F32)
        parts = []
        for jp in range(pages_per_block):
            k_rows = k_pages[jb * pages_per_block + jp][...].reshape(page_rows, HEAD_DIM).astype(BF16)
            s = lax.dot_general(q_bf, k_rows, (((1,), (1,)), ((), ())), preferred_element_type=F32)
            gate = gate + jnp.sum(jnp.where(own, s, 0.0), axis=1, keepdims=True)
            s = s + sbp_ref[newest * (1 + jp)]
            m_pg = jnp.max(s, axis=1, keepdims=True)
            p = jnp.exp(s - m_pg)
            l_pg = jnp.sum(p, axis=1, keepdims=True)
            v_rows = v_pages[jb * pages_per_block + jp][...].reshape(page_rows, HEAD_DIM).astype(BF16)
            o_pg = jnp.dot(p.astype(BF16), v_rows, preferred_element_type=F32)
            parts.append((m_pg, l_pg, o_pg))
        m_blk = functools.reduce(jnp.maximum, [pt[0] for pt in parts])
        l_blk = sum(jnp.exp(m_pg - m_blk) * l_pg for m_pg, l_pg, _ in parts)
        o_scr[n] = sum(jnp.exp(m_pg - m_blk) * o_pg for m_pg, _, o_pg in parts)
        g_scr[...] = jnp.where(lane == n, gate, g_scr[...])
        m_scr[...] = jnp.where(lane == n, m_blk, m_scr[...])
        l_scr[...] = jnp.where(lane == n, l_blk, l_scr[...])

    @pl.when(step == pl.num_programs(1) - 1)
    def _():
        g = g_scr[...]
        beaten = jnp.zeros(g.shape, jnp.int32)
        for m in range(n_blk):
            gm = g[:, m:m + 1]
            beaten = beaten + jnp.where((gm > g) | ((gm == g) & (m < lane)), 1, 0)
        selected = (beaten < MOBA_TOPK) & (lane < n_blk)
        m_sel = jnp.where(selected, m_scr[...], NEG_INF)

        s_loc = jnp.full((rows, 128), NEG_INF, F32)
        for t in range(t_new):
            col = jnp.sum(q_rows * kn_ref[t], axis=1, keepdims=True)
            s_loc = jnp.where(lane == t, col, s_loc)
        s_loc = s_loc + sbl_ref[...]

        m_tot = jnp.maximum(jnp.max(m_sel, axis=1, keepdims=True), jnp.max(s_loc, axis=1, keepdims=True))
        w_blk = jnp.where(selected, jnp.exp(m_sel - m_tot), 0.0)
        p_loc = jnp.exp(s_loc - m_tot)
        denom = jnp.sum(w_blk * l_scr[...], axis=1, keepdims=True) + jnp.sum(p_loc, axis=1, keepdims=True)
        acc = jnp.zeros((rows, HEAD_DIM), F32)
        for n in range(n_blk):
            acc = acc + w_blk[:, n:n + 1] * o_scr[n]
        for t in range(t_new):
            acc = acc + p_loc[:, t:t + 1] * vn_ref[t]
        o_ref[...] = acc / denom


def _head_rows(a, t_new):
    db = a.shape[0]
    return a.reshape(db, t_new, N_HEADS, HEAD_DIM).transpose(0, 2, 1, 3).reshape(db, N_HEADS * t_new, HEAD_DIM)


def _moba_sample(q, k_new, v_new, cache_k, cache_v, layer, page_table, sbias_pages, sbias_loc, blocks_per_step):
    db, t_new, _ = q.shape
    page = cache_k.shape[2]
    n_pages = page_table.shape[1]
    pages_per_block = MOBA_BLOCK // page
    assert MOBA_BLOCK % page == 0 and page % 8 == 0
    assert (n_pages * page) % MOBA_BLOCK == 0 and t_new <= 128
    n_blk = n_pages // pages_per_block
    assert MOBA_TOPK <= n_blk <= 128 and n_blk % blocks_per_step == 0
    assert MOBA_BLOCK + 1 >= FAR_DISTANCE
    rows = N_HEADS * t_new
    q_rows = _head_rows(q, t_new)
    kn_rows = jnp.repeat(k_new.reshape(db, t_new, N_HEADS, HEAD_DIM), t_new, axis=2)
    vn_rows = jnp.repeat(v_new.reshape(db, t_new, N_HEADS, HEAD_DIM), t_new, axis=2)
    pages_per_step = blocks_per_step * pages_per_block

    def page_spec(j):
        return pl.BlockSpec((None, None, page, N_HEADS, HEAD_DIM),
                            lambda b, s, pt, j=j: (layer, pt[b, s * pages_per_step + j], 0, 0, 0))

    row_spec = pl.BlockSpec((None, rows, HEAD_DIM), lambda b, s, pt: (b, 0, 0))
    new_spec = pl.BlockSpec((None, t_new, rows, HEAD_DIM), lambda b, s, pt: (b, 0, 0, 0))
    grid_spec = pltpu.PrefetchScalarGridSpec(
        num_scalar_prefetch=1,
        grid=(db, n_blk // blocks_per_step),
        in_specs=[row_spec, new_spec, new_spec,
                  pl.BlockSpec(sbias_pages.shape, lambda b, s, pt: (0, 0, 0)),
                  pl.BlockSpec(sbias_loc.shape, lambda b, s, pt: (0, 0))]
                 + [page_spec(j) for j in range(pages_per_step)] * 2,
        out_specs=row_spec,
        scratch_shapes=[pltpu.VMEM((rows, 128), F32), pltpu.VMEM((rows, 128), F32), pltpu.VMEM((rows, 128), F32),
                        pltpu.VMEM((n_blk, rows, HEAD_DIM), F32)],
    )
    out = pl.pallas_call(
        functools.partial(_moba_sample_kernel, blocks_per_step=blocks_per_step,
                          pages_per_block=pages_per_block, n_blk=n_blk),
        grid_spec=grid_spec,
        out_shape=jax.ShapeDtypeStruct((db, rows, HEAD_DIM), F32),
        compiler_params=pltpu.CompilerParams(dimension_semantics=("arbitrary", "arbitrary"),
                                             vmem_limit_bytes=VMEM_LIMIT_BYTES),
        name="moba_sample",
    )(page_table, q_rows, kn_rows, vn_rows, sbias_pages, sbias_loc,
      *([cache_k] * pages_per_step), *([cache_v] * pages_per_step))
    return out.reshape(db, N_HEADS, t_new, HEAD_DIM).transpose(0, 2, 1, 3).reshape(db, t_new, D_ATTN)


def _merge_kernel(x_ref, attn_ref, ga_ref, u_ref, vn_ref, gb_ref, ma_ref, mb_ref,
                  mix_ref, sb_ref, wpa_ref, wpb_ref, wo_ref, fg_ref, o_ref, *, final):
    tm = x_ref.shape[0]
    d_sgu = u_ref.shape[-1]
    group_dim = d_sgu // N_GROUPS
    first_of_pair = lax.broadcasted_iota(jnp.int32, (CHUNK, 2 * group_dim), 1) < group_dim
    z_chunks = []
    for c in range(tm // CHUNK):
        vn = vn_ref[c * CHUNK:(c + 1) * CHUNK, :].astype(BF16)
        z_pairs = []
        for j in range(N_GROUPS // 2):
            v_pair = vn[:, j * 2 * group_dim:(j + 1) * 2 * group_dim]
            z_a = jnp.dot(mix_ref[2 * j], v_pair, preferred_element_type=F32)
            z_b = jnp.dot(mix_ref[2 * j + 1], v_pair, preferred_element_type=F32)
            z_pairs.append(jnp.where(first_of_pair, z_a, z_b))
        z_chunks.append(jnp.concatenate(z_pairs, axis=1) + sb_ref[...])
    z = jnp.concatenate(z_chunks, axis=0) if len(z_chunks) > 1 else z_chunks[0]
    y_b = (u_ref[...] * z) * _silu(gb_ref[...])
    y_a = attn_ref[...] * _silu(ga_ref[...])
    p_a = jnp.dot(y_a.astype(BF16), wpa_ref[...], preferred_element_type=F32)
    p_b = jnp.dot(y_b.astype(BF16), wpb_ref[...], preferred_element_type=F32)
    mixed = jax.nn.sigmoid(ma_ref[...]) * p_a + jax.nn.sigmoid(mb_ref[...]) * p_b
    out = x_ref[...] + jnp.dot(mixed.astype(BF16), wo_ref[...], preferred_element_type=F32)
    if final:
        out = out * lax.rsqrt(jnp.mean(out * out, axis=-1, keepdims=True) + EPS) * fg_ref[...]
    o_ref[...] = out


def _merge(x, attn, ga, u, vn, gb, ma, mb, mix_bf, sgu_bias, wpa_bf, wpb_bf, wo_bf, final_g, tm, final):
    n_tok, d_model = x.shape
    d_sgu = u.shape[-1]
    assert n_tok % tm == 0 and tm % CHUNK == 0
    row = lambda i: (i, 0)
    fixed2 = lambda i: (0, 0)
    narrow = pl.BlockSpec((tm, d_sgu), row)
    wide = pl.BlockSpec((tm, d_model), row)
    return pl.pallas_call(
        functools.partial(_merge_kernel, final=final),
        grid=(n_tok // tm,),
        in_specs=[wide, pl.BlockSpec((tm, D_ATTN), row), pl.BlockSpec((tm, D_ATTN), row),
                  narrow, narrow, narrow, wide, wide,
                  pl.BlockSpec(mix_bf.shape, lambda i: (0, 0, 0)),
                  pl.BlockSpec(sgu_bias.shape, fixed2),
                  pl.BlockSpec(wpa_bf.shape, fixed2),
                  pl.BlockSpec(wpb_bf.shape, fixed2),
                  pl.BlockSpec(wo_bf.shape, fixed2),
                  pl.BlockSpec((1, d_model), fixed2)],
        out_specs=wide,
        out_shape=jax.ShapeDtypeStruct((n_tok, d_model), F32),
        compiler_params=pltpu.CompilerParams(dimension_semantics=("arbitrary",),
                                             vmem_limit_bytes=VMEM_LIMIT_BYTES),
        name="merge",
    )(x, attn, ga, u, vn, gb, ma, mb, mix_bf, sgu_bias, wpa_bf, wpb_bf, wo_bf, final_g.reshape(1, d_model))


PROMPT_TOKEN_TILE = 256
SAMPLE_BLOCKS_PER_STEP = 4


def kernel(x_prompt, x_sample, cache_k, cache_v, page_table, rel_bias, norm_g, w_in, sgu_ln_g, sgu_ln_b,
           w_s, b_s, w_pa, w_pb, w_o, final_norm_g):
    b, s, d_model = x_prompt.shape
    db, t_new, _ = x_sample.shape
    depth = w_in.shape[0]
    d_sgu = sgu_ln_g.shape[-1]
    group_dim = d_sgu // N_GROUPS
    assert s % CHUNK == 0 and (db * t_new) % CHUNK == 0 and CHUNK % t_new == 0
    assert cache_k.shape[3] == N_HEADS and cache_k.shape[4] == HEAD_DIM

    bias_tiles = _prompt_bias_tiles(rel_bias)
    sbias_pages, sbias_loc = _sample_bias(rel_bias, t_new, cache_k.shape[2])

    xp = x_prompt.reshape(b * s, d_model)
    xs = x_sample.reshape(db * t_new, d_model)
    k_p, v_p, k_s, v_s, sgu_rows = [], [], [], [], []
    for l in range(depth):
        final = l == depth - 1
        w_bf = w_in[l].astype(BF16)
        wpa_bf, wpb_bf, wo_bf = w_pa[l].astype(BF16), w_pb[l].astype(BF16), w_o[l].astype(BF16)
        mix_p = jnp.tril(w_s[l]).astype(BF16)
        bias_p = jnp.repeat(b_s[l].T, group_dim, axis=1)
        corner = jnp.tril(w_s[l][:, :t_new, :t_new])
        eye = jnp.eye(CHUNK // t_new, dtype=F32)
        mix_s = jnp.einsum("ab,gts->gatbs", eye, corner).reshape(N_GROUPS, CHUNK, CHUNK).astype(BF16)
        bias_s = jnp.tile(jnp.repeat(b_s[l][:, :t_new].T, group_dim, axis=1), (CHUNK // t_new, 1))

        q, k, v, ga, u, vn, gb, ma, mb = _inproj(xp, norm_g[l], w_bf, sgu_ln_g[l], sgu_ln_b[l], PROMPT_TOKEN_TILE)
        attn = _moba_prompt(q.reshape(b, s, D_ATTN), k.reshape(b, s, D_ATTN), v.reshape(b, s, D_ATTN), bias_tiles)
        xp = _merge(xp, attn.reshape(b * s, D_ATTN), ga, u, vn, gb, ma, mb, mix_p, bias_p,
                    wpa_bf, wpb_bf, wo_bf, final_norm_g, PROMPT_TOKEN_TILE, final)
        k_p.append(k.reshape(b, s, N_HEADS, HEAD_DIM))
        v_p.append(v.reshape(b, s, N_HEADS, HEAD_DIM))

        q, k, v, ga, u, vn, gb, ma, mb = _inproj(xs, norm_g[l], w_bf, sgu_ln_g[l], sgu_ln_b[l], CHUNK)
        attn = _moba_sample(q.reshape(db, t_new, D_ATTN), k.reshape(db, t_new, D_ATTN), v.reshape(db, t_new, D_ATTN),
                            cache_k, cache_v, l, page_table, sbias_pages, sbias_loc, SAMPLE_BLOCKS_PER_STEP)
        xs = _merge(xs, attn.reshape(db * t_new, D_ATTN), ga, u, vn, gb, ma, mb, mix_s, bias_s,
                    wpa_bf, wpb_bf, wo_bf, final_norm_g, CHUNK, final)
        k_s.append(k.reshape(db, t_new, N_HEADS, HEAD_DIM))
        v_s.append(v.reshape(db, t_new, N_HEADS, HEAD_DIM))
        sgu_rows.append(vn.reshape(db, t_new, d_sgu))

    return (xp.reshape(b, s, d_model), xs.reshape(db, t_new, d_model),
            jnp.stack(k_p), jnp.stack(v_p), jnp.stack(k_s), jnp.stack(v_s), jnp.stack(sgu_rows))
```

```python
import functools
import math

import numpy as np
import jax
import jax.numpy as jnp
from jax import lax
from jax.experimental import pallas as pl
from jax.experimental.pallas import tpu as pltpu

N_HEADS = 8
HEAD_DIM = 64
D_ATTN = N_HEADS * HEAD_DIM
MOBA_BLOCK = 256
MOBA_TOPK = 3
N_GROUPS = 8
CHUNK = 128
N_BUCKETS = 32
MAX_DISTANCE = 128
EPS = 1e-6
LN_EPS = 1e-5
SCALE = HEAD_DIM ** -0.5

F32 = jnp.float32
BF16 = jnp.bfloat16
NEG_INF = float("-inf")
NT_DIMS = (((1,), (1,)), ((), ()))

VMEM_LIMIT_BYTES = 56 * 1024 * 1024


def _bucket_thresholds():
    max_exact = N_BUCKETS // 2
    d = np.arange(max_exact, 4 * MAX_DISTANCE, dtype=np.float64)
    log_b = max_exact + (np.log(d / max_exact) / math.log(MAX_DISTANCE / max_exact) * (N_BUCKETS - max_exact)).astype(np.int64)
    bucket = np.minimum(log_b, N_BUCKETS - 1)
    return [int(d[np.argmax(bucket >= b)]) for b in range(max_exact + 1, N_BUCKETS)]


BUCKET_THRESHOLDS = _bucket_thresholds()
FAR_DISTANCE = BUCKET_THRESHOLDS[-1]


def _rel_bucket(dist):
    max_exact = N_BUCKETS // 2
    log_b = jnp.full(dist.shape, max_exact, jnp.int32)
    for thr in BUCKET_THRESHOLDS:
        log_b = log_b + jnp.where(dist >= thr, 1, 0)
    return jnp.where(dist < max_exact, dist, log_b)


def _gelu_tanh(x):
    return 0.5 * x * (1.0 + jnp.tanh(math.sqrt(2.0 / math.pi) * (x + 0.044715 * (x * x * x))))


def _silu(x):
    return x * jax.nn.sigmoid(x)


def _prompt_bias_kernel(tab_ref, o_ref):
    h = pl.program_id(0)
    c = pl.program_id(1)
    shape = o_ref.shape
    key = lax.broadcasted_iota(jnp.int32, shape, 0)
    qry = lax.broadcasted_iota(jnp.int32, shape, 1)
    dist = c * MOBA_BLOCK + qry - key
    bucket = _rel_bucket(dist)
    val = jnp.full(shape, tab_ref[N_BUCKETS - 1, h], F32)
    for b in range(N_BUCKETS - 1):
        val = jnp.where(bucket == b, tab_ref[b, h], val)
    o_ref[...] = jnp.where(dist < 0, NEG_INF, val)


def _prompt_bias_tiles(rel_bias):
    assert MOBA_BLOCK + 1 >= FAR_DISTANCE
    return pl.pallas_call(
        _prompt_bias_kernel,
        grid=(N_HEADS, 2),
        in_specs=[pl.BlockSpec(memory_space=pltpu.SMEM)],
        out_specs=pl.BlockSpec((None, None, MOBA_BLOCK, MOBA_BLOCK), lambda h, c: (h, c, 0, 0)),
        out_shape=jax.ShapeDtypeStruct((N_HEADS, 2, MOBA_BLOCK, MOBA_BLOCK), F32),
        compiler_params=pltpu.CompilerParams(dimension_semantics=("arbitrary", "arbitrary")),
        name="prompt_bias_tiles",
    )(rel_bias)


def _sample_bias_kernel(tab_ref, o_ref, *, t_new):
    shape = o_ref.shape
    row = lax.broadcasted_iota(jnp.int32, shape, 0)
    lane = lax.broadcasted_iota(jnp.int32, shape, 1)
    t = row % t_new
    dist = jnp.where(lane < MOBA_BLOCK, MOBA_BLOCK + t - lane,
                     jnp.where(lane < MOBA_BLOCK + 128, t - (lane - MOBA_BLOCK), FAR_DISTANCE))
    bucket = _rel_bucket(dist)
    tab = tab_ref[...]
    val = jnp.broadcast_to(tab[:, N_BUCKETS - 1:N_BUCKETS], shape)
    for b in range(N_BUCKETS - 1):
        val = jnp.where(bucket == b, tab[:, b:b + 1], val)
    o_ref[...] = jnp.where(dist < 0, NEG_INF, val)


def _sample_bias(rel_bias, t_new):
    rows = N_HEADS * t_new
    tab_rows = jnp.repeat(rel_bias.T, t_new, axis=0)
    return pl.pallas_call(
        functools.partial(_sample_bias_kernel, t_new=t_new),
        out_shape=jax.ShapeDtypeStruct((rows, MOBA_BLOCK + 256), F32),
        name="sample_bias",
    )(tab_rows)


def _inproj_kernel(*refs, transposed_qkv):
    if transposed_qkv:
        (x_ref, g_ref, wqkv_ref, w_ref, lng_ref, lnb_ref, _, _,
         q_ref, k_ref, v_ref, ga_ref, u_ref, vn_ref, gb_ref, ma_ref, mb_ref) = refs
    else:
        (x_ref, g_ref, wqkv_ref, w_ref, lng_ref, lnb_ref,
         q_ref, k_ref, v_ref, ga_ref, u_ref, vn_ref, gb_ref, ma_ref, mb_ref) = refs
    x = x_ref[...]
    d_model = x.shape[-1]
    d_sgu = u_ref.shape[-1]
    h = (x * lax.rsqrt(jnp.mean(x * x, axis=-1, keepdims=True) + EPS) * g_ref[...]).astype(BF16)

    if transposed_qkv:
        qkv_t = lax.dot_general(wqkv_ref[...], h, NT_DIMS, preferred_element_type=F32)
        q_ref[...] = qkv_t[0:D_ATTN]
        k_ref[...] = qkv_t[D_ATTN:2 * D_ATTN]
        v_ref[...] = qkv_t[2 * D_ATTN:3 * D_ATTN]
    else:
        qkv = jnp.dot(h, wqkv_ref[...], preferred_element_type=F32)
        q_ref[...] = qkv[:, 0:D_ATTN]
        k_ref[...] = qkv[:, D_ATTN:2 * D_ATTN]
        v_ref[...] = qkv[:, 2 * D_ATTN:3 * D_ATTN]

    def seg(lo, width):
        return jnp.dot(h, w_ref[:, lo:lo + width], preferred_element_type=F32)

    ga_ref[...] = seg(0, D_ATTN)
    base = D_ATTN
    u_ref[...] = _gelu_tanh(seg(base, d_sgu))
    vs = _gelu_tanh(seg(base + d_sgu, d_sgu))
    mu = jnp.mean(vs, axis=-1, keepdims=True)
    var = jnp.mean(jnp.square(vs - mu), axis=-1, keepdims=True)
    vn_ref[...] = (vs - mu) * lax.rsqrt(var + LN_EPS) * lng_ref[...] + lnb_ref[...]
    gb_ref[...] = seg(base + 2 * d_sgu, d_sgu)
    ma_ref[...] = seg(base + 3 * d_sgu, d_model)
    mb_ref[...] = seg(base + 3 * d_sgu + d_model, d_model)


def _inproj(x, norm_g, wqkv_bf, w_bf, ln_g, ln_b, tm, kv_stack=None, layer=None, seq_len=None):
    n_tok, d_model = x.shape
    d_sgu = ln_g.shape[-1]
    assert n_tok % tm == 0
    widths = [D_ATTN] + [d_sgu] * 3 + [d_model] * 2
    assert sum(widths) == w_bf.shape[1]
    row = lambda i: (i, 0)
    fixed = lambda i: (0, 0)
    transposed = kv_stack is not None
    in_specs = [pl.BlockSpec((tm, d_model), row),
                pl.BlockSpec((1, d_model), fixed),
                pl.BlockSpec(wqkv_bf.shape, fixed),
                pl.BlockSpec(w_bf.shape, fixed),
                pl.BlockSpec((1, d_sgu), fixed),
                pl.BlockSpec((1, d_sgu), fixed)]
    args = [x, norm_g.reshape(1, d_model), wqkv_bf, w_bf, ln_g.reshape(1, d_sgu), ln_b.reshape(1, d_sgu)]
    rest_specs = [pl.BlockSpec((tm, w), row) for w in widths]
    rest_shapes = [jax.ShapeDtypeStruct((n_tok, w), F32) for w in widths]
    aliases = {}
    if transposed:
        assert seq_len % tm == 0 and n_tok % seq_len == 0
        tiles = seq_len // tm
        batch = n_tok // seq_len
        in_specs += [pl.BlockSpec(memory_space=pl.ANY)] * 2
        args += list(kv_stack)
        aliases = {6: 1, 7: 2}
        stack_spec = pl.BlockSpec((None, None, D_ATTN, tm), lambda i: (layer, i // tiles, 0, i % tiles))
        qkv_specs = [pl.BlockSpec((None, D_ATTN, tm), lambda i: (i // tiles, 0, i % tiles)), stack_spec, stack_spec]
        qkv_shapes = [jax.ShapeDtypeStruct((batch, D_ATTN, seq_len), F32),
                      jax.ShapeDtypeStruct(kv_stack[0].shape, F32), jax.ShapeDtypeStruct(kv_stack[1].shape, F32)]
    else:
        qkv_specs = [pl.BlockSpec((tm, D_ATTN), row)] * 3
        qkv_shapes = [jax.ShapeDtypeStruct((n_tok, D_ATTN), F32)] * 3
    return pl.pallas_call(
        functools.partial(_inproj_kernel, transposed_qkv=transposed),
        grid=(n_tok // tm,),
        in_specs=in_specs,
        out_specs=qkv_specs + rest_specs,
        out_shape=qkv_shapes + rest_shapes,
        input_output_aliases=aliases,
        compiler_params=pltpu.CompilerParams(dimension_semantics=("arbitrary",),
                                             vmem_limit_bytes=VMEM_LIMIT_BYTES),
        name="inproj",
    )(*args)


def _moba_prompt_kernel(tab_ref, qt_ref, kt_ref, vt_ref, bias_ref, o_ref,
                        kb_scr, vb_scr, kmbd_scr, neg_scr, qh_scr, m_scr, l_scr, acc_scr):
    qt = pl.program_id(1)
    n_blk = kb_scr.shape[0]
    tq = qt_ref.shape[1]

    @pl.when(qt == 0)
    def _():
        lane_head = lax.broadcasted_iota(jnp.int32, (1, D_ATTN), 1) // HEAD_DIM
        for n in range(n_blk):
            kblk = kt_ref[:, n * MOBA_BLOCK:(n + 1) * MOBA_BLOCK].T
            kb_scr[n] = kblk.astype(BF16)
            vb_scr[n] = vt_ref[:, n * MOBA_BLOCK:(n + 1) * MOBA_BLOCK].astype(BF16)
            k_mean = jnp.sum(kblk, axis=0, keepdims=True) * (1.0 / MOBA_BLOCK)
            for h in range(N_HEADS):
                kmbd_scr[h * n_blk + n:h * n_blk + n + 1, :] = jnp.where(lane_head == h, k_mean, 0.0)

    q_t = (qt_ref[...] * SCALE).astype(BF16)

    gate = jnp.dot(kmbd_scr[...].astype(BF16), q_t, preferred_element_type=F32)
    blk = lax.broadcasted_iota(jnp.int32, (n_blk, tq), 0)
    half = lax.broadcasted_iota(jnp.int32, (2 * HEAD_DIM, tq), 0) // HEAD_DIM
    for h in range(N_HEADS):
        g = gate[h * n_blk:(h + 1) * n_blk, :]
        beaten = jnp.zeros((n_blk, tq), jnp.int32)
        for m in range(n_blk):
            gm = g[m:m + 1, :]
            wins = (gm > g) | ((gm == g) & (m < blk))
            beaten = beaten + jnp.where(wins, 1, 0) * (m < qt).astype(jnp.int32)
        selected = (blk < qt) & (beaten < MOBA_TOPK)
        far = jnp.where(blk < qt - 1, tab_ref[N_BUCKETS - 1, h], 0.0)
        neg_scr[h] = jnp.where(selected, far, NEG_INF)
        pair = h // 2
        q_pair = q_t[pair * 2 * HEAD_DIM:(pair + 1) * 2 * HEAD_DIM, :]
        qh_scr[h] = jnp.where(half == h % 2, q_pair, jnp.zeros_like(q_pair))

    def scores(h, n):
        pair = h // 2
        k_pair = kb_scr[n, :, pair * 2 * HEAD_DIM:(pair + 1) * 2 * HEAD_DIM]
        return jnp.dot(k_pair, qh_scr[h], preferred_element_type=F32)

    def weighted_v(h, n, p):
        v_t = vb_scr[n, h * HEAD_DIM:(h + 1) * HEAD_DIM, :]
        return jnp.dot(v_t, p.astype(BF16), preferred_element_type=F32)

    def first_block(h):
        s = scores(h, qt) + bias_ref[h, 0]
        m_new = jnp.max(s, axis=0, keepdims=True)
        p = jnp.exp(s - m_new)
        m_scr[h] = m_new
        l_scr[h] = jnp.sum(p, axis=0, keepdims=True)
        acc_scr[h * HEAD_DIM:(h + 1) * HEAD_DIM, :] = weighted_v(h, qt, p)

    def later_block(h, n, s):
        m_old = m_scr[h]
        m_new = jnp.maximum(m_old, jnp.max(s, axis=0, keepdims=True))
        alpha = jnp.exp(m_old - m_new)
        p = jnp.exp(s - m_new)
        m_scr[h] = m_new
        l_scr[h] = alpha * l_scr[h] + jnp.sum(p, axis=0, keepdims=True)
        rows = slice(h * HEAD_DIM, (h + 1) * HEAD_DIM)
        acc_scr[rows, :] = alpha * acc_scr[rows, :] + weighted_v(h, n, p)

    for h in range(N_HEADS):
        first_block(h)

    @pl.when(qt >= 1)
    def _():
        n = qt - 1
        for h in range(N_HEADS):
            later_block(h, n, scores(h, n) + bias_ref[h, 1] + neg_scr[h, pl.ds(n, 1), :])

    def far_body(n, carry):
        for h in range(N_HEADS):
            later_block(h, n, scores(h, n) + neg_scr[h, pl.ds(n, 1), :])
        return carry

    lax.fori_loop(0, jnp.maximum(qt - 1, 0), far_body, 0)

    for h in range(N_HEADS):
        rows = slice(h * HEAD_DIM, (h + 1) * HEAD_DIM)
        acc_scr[rows, :] = acc_scr[rows, :] / l_scr[h]
    o_ref[...] = acc_scr[...].T


def _moba_prompt(rel_bias, q_t, k_all, v_all, layer, bias_tiles):
    b, _, s = q_t.shape
    assert s % MOBA_BLOCK == 0
    n_blk = s // MOBA_BLOCK
    tq = MOBA_BLOCK
    stack_spec = pl.BlockSpec((None, None, D_ATTN, s), lambda i, j: (layer, i, 0, 0))
    return pl.pallas_call(
        _moba_prompt_kernel,
        grid=(b, s // tq),
        in_specs=[pl.BlockSpec(memory_space=pltpu.SMEM),
                  pl.BlockSpec((None, D_ATTN, tq), lambda i, j: (i, 0, j)),
                  stack_spec, stack_spec,
                  pl.BlockSpec(bias_tiles.shape, lambda i, j: (0, 0, 0, 0))],
        out_specs=pl.BlockSpec((None, tq, D_ATTN), lambda i, j: (i, j, 0)),
        out_shape=jax.ShapeDtypeStruct((b, s, D_ATTN), F32),
        scratch_shapes=[pltpu.VMEM((n_blk, MOBA_BLOCK, D_ATTN), BF16),
                        pltpu.VMEM((n_blk, D_ATTN, MOBA_BLOCK), BF16),
                        pltpu.VMEM((N_HEADS * n_blk, D_ATTN), F32),
                        pltpu.VMEM((N_HEADS, n_blk, tq), F32),
                        pltpu.VMEM((N_HEADS, 2 * HEAD_DIM, tq), BF16),
                        pltpu.VMEM((N_HEADS, 1, tq), F32),
                        pltpu.VMEM((N_HEADS, 1, tq), F32),
                        pltpu.VMEM((D_ATTN, tq), F32)],
        compiler_params=pltpu.CompilerParams(dimension_semantics=("arbitrary", "arbitrary"),
                                             vmem_limit_bytes=VMEM_LIMIT_BYTES),
        name="moba_prompt",
    )(rel_bias, q_t, k_all, v_all, bias_tiles)


def _moba_sample_kernel(pt_ref, q_ref, kn_ref, vn_ref, sb_ref, *refs, blocks_per_step, pages_per_block, n_blk):
    n_pages = blocks_per_step * pages_per_block
    k_pages = refs[:n_pages]
    v_pages = refs[n_pages:2 * n_pages]
    o_ref = refs[2 * n_pages]
    g_scr, m_scr, l_scr, o_scr = refs[2 * n_pages + 1:]
    del pt_ref
    step = pl.program_id(1)
    t_new = kn_ref.shape[0]
    rows = q_ref.shape[0]
    page = k_pages[0].shape[-1]

    row_head = lax.broadcasted_iota(jnp.int32, (rows, D_ATTN), 0) // t_new
    lane_head = lax.broadcasted_iota(jnp.int32, (rows, D_ATTN), 1) // HEAD_DIM
    own_lanes = row_head == lane_head
    q_rows = jnp.where(own_lanes, q_ref[...] * SCALE, 0.0)
    q_bf = q_rows.astype(BF16)
    lane = lax.broadcasted_iota(jnp.int32, (rows, 128), 1)

    @pl.when(step == 0)
    def _():
        g_scr[...] = jnp.full(g_scr.shape, NEG_INF, F32)
        m_scr[...] = jnp.full(m_scr.shape, NEG_INF, F32)
        l_scr[...] = jnp.zeros(l_scr.shape, F32)

    far_bias = sb_ref[:, MOBA_BLOCK + 128:MOBA_BLOCK + 129]
    for jb in range(blocks_per_step):
        n = step * blocks_per_step + jb
        s_parts = []
        for jp in range(pages_per_block):
            k_t = k_pages[jb * pages_per_block + jp][...].reshape(D_ATTN, page).astype(BF16)
            s_parts.append(jnp.dot(q_bf, k_t, preferred_element_type=F32))
        gate = sum(jnp.sum(sp, axis=1, keepdims=True) for sp in s_parts)
        newest = (jnp.zeros((rows, page), jnp.int32) + n) == n_blk - 1
        s_parts = [sp + jnp.where(newest, sb_ref[:, jp * page:(jp + 1) * page], far_bias)
                   for jp, sp in enumerate(s_parts)]
        m_blk = functools.reduce(jnp.maximum, [jnp.max(sp, axis=1, keepdims=True) for sp in s_parts])
        p_parts = [jnp.exp(sp - m_blk) for sp in s_parts]
        l_blk = sum(jnp.sum(pp, axis=1, keepdims=True) for pp in p_parts)
        o_blk = sum(lax.dot_general(pp.astype(BF16),
                                    v_pages[jb * pages_per_block + jp][...].reshape(D_ATTN, page).astype(BF16),
                                    NT_DIMS, preferred_element_type=F32)
                    for jp, pp in enumerate(p_parts))
        o_scr[n] = o_blk
        g_scr[...] = jnp.where(lane == n, gate, g_scr[...])
        m_scr[...] = jnp.where(lane == n, m_blk, m_scr[...])
        l_scr[...] = jnp.where(lane == n, l_blk, l_scr[...])

    @pl.when(step == pl.num_programs(1) - 1)
    def _():
        g = g_scr[...]
        beaten = jnp.zeros(g.shape, jnp.int32)
        for m in range(n_blk):
            gm = g[:, m:m + 1]
            beaten = beaten + jnp.where((gm > g) | ((gm == g) & (m < lane)), 1, 0)
        selected = (beaten < MOBA_TOPK) & (lane < n_blk)
        m_sel = jnp.where(selected, m_scr[...], NEG_INF)

        s_loc = jnp.full((rows, 128), NEG_INF, F32)
        for t in range(t_new):
            col = jnp.sum(q_rows * kn_ref[t:t + 1, :], axis=1, keepdims=True)
            s_loc = jnp.where(lane == t, col, s_loc)
        s_loc = s_loc + sb_ref[:, MOBA_BLOCK:MOBA_BLOCK + 128]

        m_tot = jnp.maximum(jnp.max(m_sel, axis=1, keepdims=True), jnp.max(s_loc, axis=1, keepdims=True))
        w_blk = jnp.where(selected, jnp.exp(m_sel - m_tot), 0.0)
        p_loc = jnp.exp(s_loc - m_tot)
        denom = jnp.sum(w_blk * l_scr[...], axis=1, keepdims=True) + jnp.sum(p_loc, axis=1, keepdims=True)
        acc = jnp.zeros((rows, D_ATTN), F32)
        for n in range(n_blk):
            acc = acc + w_blk[:, n:n + 1] * o_scr[n]
        for t in range(t_new):
            acc = acc + p_loc[:, t:t + 1] * vn_ref[t:t + 1, :]
        acc = jnp.where(own_lanes, acc / denom, 0.0)
        folded = acc
        for h in range(1, N_HEADS):
            folded = folded + pltpu.roll(acc, rows - h * t_new, 0)
        o_ref[...] = folded[:t_new, :]


def _moba_sample(q, k_new, v_new, cache_kt, cache_vt, layer, page_table, sbias, blocks_per_step):
    db, t_new, _ = q.shape
    page = cache_kt.shape[-1]
    n_pages = page_table.shape[1]
    pages_per_block = MOBA_BLOCK // page
    assert page == 128 and MOBA_BLOCK % page == 0
    assert (n_pages * page) % MOBA_BLOCK == 0 and t_new <= 128
    n_blk = n_pages // pages_per_block
    assert MOBA_TOPK <= n_blk <= 128 and n_blk % blocks_per_step == 0
    assert MOBA_BLOCK + 1 >= FAR_DISTANCE
    rows = N_HEADS * t_new
    q_rows = jnp.tile(q, (1, N_HEADS, 1))
    pages_per_step = blocks_per_step * pages_per_block

    def page_spec(j):
        return pl.BlockSpec((None, None, N_HEADS, HEAD_DIM, page),
                            lambda b, s, pt, j=j: (layer, pt[b, s * pages_per_step + j], 0, 0, 0))

    tok_spec = pl.BlockSpec((None, t_new, D_ATTN), lambda b, s, pt: (b, 0, 0))
    grid_spec = pltpu.PrefetchScalarGridSpec(
        num_scalar_prefetch=1,
        grid=(db, n_blk // blocks_per_step),
        in_specs=[pl.BlockSpec((None, rows, D_ATTN), lambda b, s, pt: (b, 0, 0)), tok_spec, tok_spec,
                  pl.BlockSpec(sbias.shape, lambda b, s, pt: (0, 0))]
                 + [page_spec(j) for j in range(pages_per_step)] * 2,
        out_specs=tok_spec,
        scratch_shapes=[pltpu.VMEM((rows, 128), F32), pltpu.VMEM((rows, 128), F32), pltpu.VMEM((rows, 128), F32),
                        pltpu.VMEM((n_blk, rows, D_ATTN), F32)],
    )
    return pl.pallas_call(
        functools.partial(_moba_sample_kernel, blocks_per_step=blocks_per_step,
                          pages_per_block=pages_per_block, n_blk=n_blk),
        grid_spec=grid_spec,
        out_shape=jax.ShapeDtypeStruct((db, t_new, D_ATTN), F32),
        compiler_params=pltpu.CompilerParams(dimension_semantics=("arbitrary", "arbitrary"),
                                             vmem_limit_bytes=VMEM_LIMIT_BYTES),
        name="moba_sample",
    )(page_table, q_rows, k_new, v_new, sbias, *([cache_kt] * pages_per_step), *([cache_vt] * pages_per_step))


def _merge_kernel(x_ref, attn_ref, ga_ref, u_ref, vn_ref, gb_ref, ma_ref, mb_ref,
                  mix_ref, sb_ref, wpa_ref, wpb_ref, wo_ref, fg_ref, o_ref, *, final):
    tm = x_ref.shape[0]
    d_sgu = u_ref.shape[-1]
    group_dim = d_sgu // N_GROUPS
    first_of_pair = lax.broadcasted_iota(jnp.int32, (CHUNK, 2 * group_dim), 1) < group_dim
    z_chunks = []
    for c in range(tm // CHUNK):
        vn = vn_ref[c * CHUNK:(c + 1) * CHUNK, :].astype(BF16)
        z_pairs = []
        for j in range(N_GROUPS // 2):
            v_pair = vn[:, j * 2 * group_dim:(j + 1) * 2 * group_dim]
            z_a = jnp.dot(mix_ref[2 * j], v_pair, preferred_element_type=F32)
            z_b = jnp.dot(mix_ref[2 * j + 1], v_pair, preferred_element_type=F32)
            z_pairs.append(jnp.where(first_of_pair, z_a, z_b))
        z_chunks.append(jnp.concatenate(z_pairs, axis=1) + sb_ref[...])
    z = jnp.concatenate(z_chunks, axis=0) if len(z_chunks) > 1 else z_chunks[0]
    y_b = (u_ref[...] * z) * _silu(gb_ref[...])
    y_a = attn_ref[...] * _silu(ga_ref[...])
    p_a = jnp.dot(y_a.astype(BF16), wpa_ref[...], preferred_element_type=F32)
    p_b = jnp.dot(y_b.astype(BF16), wpb_ref[...], preferred_element_type=F32)
    mixed = jax.nn.sigmoid(ma_ref[...]) * p_a + jax.nn.sigmoid(mb_ref[...]) * p_b
    out = x_ref[...] + jnp.dot(mixed.astype(BF16), wo_ref[...], preferred_element_type=F32)
    if final:
        out = out * lax.rsqrt(jnp.mean(out * out, axis=-1, keepdims=True) + EPS) * fg_ref[...]
    o_ref[...] = out


def _merge(x, attn, ga, u, vn, gb, ma, mb, mix_bf, sgu_bias, wpa_bf, wpb_bf, wo_bf, final_g, tm, final):
    n_tok, d_model = x.shape
    d_sgu = u.shape[-1]
    assert n_tok % tm == 0 and tm % CHUNK == 0
    row = lambda i: (i, 0)
    fixed2 = lambda i: (0, 0)
    narrow = pl.BlockSpec((tm, d_sgu), row)
    wide = pl.BlockSpec((tm, d_model), row)
    return pl.pallas_call(
        functools.partial(_merge_kernel, final=final),
        grid=(n_tok // tm,),
        in_specs=[wide, pl.BlockSpec((tm, D_ATTN), row), pl.BlockSpec((tm, D_ATTN), row),
                  narrow, narrow, narrow, wide, wide,
                  pl.BlockSpec(mix_bf.shape, lambda i: (0, 0, 0)),
                  pl.BlockSpec(sgu_bias.shape, fixed2),
                  pl.BlockSpec(wpa_bf.shape, fixed2),
                  pl.BlockSpec(wpb_bf.shape, fixed2),
                  pl.BlockSpec(wo_bf.shape, fixed2),
                  pl.BlockSpec((1, d_model), fixed2)],
        out_specs=wide,
        out_shape=jax.ShapeDtypeStruct((n_tok, d_model), F32),
        compiler_params=pltpu.CompilerParams(dimension_semantics=("arbitrary",),
                                             vmem_limit_bytes=VMEM_LIMIT_BYTES),
        name="merge",
    )(x, attn, ga, u, vn, gb, ma, mb, mix_bf, sgu_bias, wpa_bf, wpb_bf, wo_bf, final_g.reshape(1, d_model))


PROMPT_TOKEN_TILE = 256
SAMPLE_BLOCKS_PER_STEP = 4


def kernel(x_prompt, x_sample, cache_k, cache_v, page_table, rel_bias, norm_g, w_in, sgu_ln_g, sgu_ln_b,
           w_s, b_s, w_pa, w_pb, w_o, final_norm_g):
    b, s, d_model = x_prompt.shape
    db, t_new, _ = x_sample.shape
    depth = w_in.shape[0]
    d_sgu = sgu_ln_g.shape[-1]
    group_dim = d_sgu // N_GROUPS
    assert s % CHUNK == 0 and (db * t_new) % CHUNK == 0 and CHUNK % t_new == 0
    assert cache_k.shape[3] == N_HEADS and cache_k.shape[4] == HEAD_DIM

    bias_tiles = _prompt_bias_tiles(rel_bias)
    sbias = _sample_bias(rel_bias, t_new)
    cache_kt = jnp.transpose(cache_k, (0, 1, 3, 4, 2))
    cache_vt = jnp.transpose(cache_v, (0, 1, 3, 4, 2))

    xp = x_prompt.reshape(b * s, d_model)
    xs = x_sample.reshape(db * t_new, d_model)
    k_all = jnp.zeros((depth, b, D_ATTN, s), F32)
    v_all = jnp.zeros((depth, b, D_ATTN, s), F32)
    k_s, v_s, sgu_rows = [], [], []
    for l in range(depth):
        final = l == depth - 1
        wqkv = w_in[l][:, :3 * D_ATTN]
        wqkv_bf, wqkv_t_bf = wqkv.astype(BF16), wqkv.T.astype(BF16)
        w_bf = w_in[l][:, 3 * D_ATTN:].astype(BF16)
        wpa_bf, wpb_bf, wo_bf = w_pa[l].astype(BF16), w_pb[l].astype(BF16), w_o[l].astype(BF16)
        mix_p = jnp.tril(w_s[l]).astype(BF16)
        bias_p = jnp.repeat(b_s[l].T, group_dim, axis=1)
        corner = jnp.tril(w_s[l][:, :t_new, :t_new])
        eye = jnp.eye(CHUNK // t_new, dtype=F32)
        mix_s = jnp.einsum("ab,gts->gatbs", eye, corner).reshape(N_GROUPS, CHUNK, CHUNK).astype(BF16)
        bias_s = jnp.tile(jnp.repeat(b_s[l][:, :t_new].T, group_dim, axis=1), (CHUNK // t_new, 1))

        q_t, k_all, v_all, ga, u, vn, gb, ma, mb = _inproj(
            xp, norm_g[l], wqkv_t_bf, w_bf, sgu_ln_g[l], sgu_ln_b[l], PROMPT_TOKEN_TILE,
            kv_stack=(k_all, v_all), layer=l, seq_len=s)
        attn = _moba_prompt(rel_bias, q_t, k_all, v_all, l, bias_tiles)
        xp = _merge(xp, attn.reshape(b * s, D_ATTN), ga, u, vn, gb, ma, mb, mix_p, bias_p,
                    wpa_bf, wpb_bf, wo_bf, final_norm_g, PROMPT_TOKEN_TILE, final)

        q, k, v, ga, u, vn, gb, ma, mb = _inproj(xs, norm_g[l], wqkv_bf, w_bf, sgu_ln_g[l], sgu_ln_b[l], CHUNK)
        attn = _moba_sample(q.reshape(db, t_new, D_ATTN), k.reshape(db, t_new, D_ATTN), v.reshape(db, t_new, D_ATTN),
                            cache_kt, cache_vt, l, page_table, sbias, SAMPLE_BLOCKS_PER_STEP)
        xs = _merge(xs, attn.reshape(db * t_new, D_ATTN), ga, u, vn, gb, ma, mb, mix_s, bias_s,
                    wpa_bf, wpb_bf, wo_bf, final_norm_g, CHUNK, final)
        k_s.append(k.reshape(db, t_new, N_HEADS, HEAD_DIM))
        v_s.append(v.reshape(db, t_new, N_HEADS, HEAD_DIM))
        sgu_rows.append(vn.reshape(db, t_new, d_sgu))

    k_prompt = jnp.transpose(k_all.reshape(depth, b, N_HEADS, HEAD_DIM, s), (0, 1, 4, 2, 3))
    v_prompt = jnp.transpose(v_all.reshape(depth, b, N_HEADS, HEAD_DIM, s), (0, 1, 4, 2, 3))
    return (xp.reshape(b, s, d_model), xs.reshape(db, t_new, d_model),
            k_prompt, v_prompt, jnp.stack(k_s), jnp.stack(v_s), jnp.stack(sgu_rows))
```

```python
import functools
import math

import numpy as np
import jax
import jax.numpy as jnp
from jax import lax
from jax.experimental import pallas as pl
from jax.experimental.pallas import tpu as pltpu

N_HEADS = 8
HEAD_DIM = 64
D_ATTN = N_HEADS * HEAD_DIM
MOBA_BLOCK = 256
MOBA_TOPK = 3
N_GROUPS = 8
CHUNK = 128
N_BUCKETS = 32
MAX_DISTANCE = 128
EPS = 1e-6
LN_EPS = 1e-5
SCALE = HEAD_DIM ** -0.5
LOG2_E = math.log2(math.e)

F32 = jnp.float32
BF16 = jnp.bfloat16
NEG_INF = float("-inf")
NT_DIMS = (((1,), (1,)), ((), ()))

VMEM_LIMIT_BYTES = 56 * 1024 * 1024


def _bucket_thresholds():
    max_exact = N_BUCKETS // 2
    d = np.arange(max_exact, 4 * MAX_DISTANCE, dtype=np.float64)
    log_b = max_exact + (np.log(d / max_exact) / math.log(MAX_DISTANCE / max_exact) * (N_BUCKETS - max_exact)).astype(np.int64)
    bucket = np.minimum(log_b, N_BUCKETS - 1)
    return [int(d[np.argmax(bucket >= b)]) for b in range(max_exact + 1, N_BUCKETS)]


BUCKET_THRESHOLDS = _bucket_thresholds()
FAR_DISTANCE = BUCKET_THRESHOLDS[-1]


def _rel_bucket(dist):
    max_exact = N_BUCKETS // 2
    log_b = jnp.full(dist.shape, max_exact, jnp.int32)
    for thr in BUCKET_THRESHOLDS:
        log_b = log_b + jnp.where(dist >= thr, 1, 0)
    return jnp.where(dist < max_exact, dist, log_b)


def _gelu_tanh(x):
    return 0.5 * x * (1.0 + jnp.tanh(math.sqrt(2.0 / math.pi) * (x + 0.044715 * (x * x * x))))


def _silu(x):
    return x * jax.nn.sigmoid(x)


def _prompt_bias_kernel(tab_ref, o_ref):
    h = pl.program_id(0)
    c = pl.program_id(1)
    shape = o_ref.shape
    key = lax.broadcasted_iota(jnp.int32, shape, 0)
    qry = lax.broadcasted_iota(jnp.int32, shape, 1)
    dist = c * MOBA_BLOCK + qry - key
    bucket = _rel_bucket(dist)
    val = jnp.full(shape, tab_ref[N_BUCKETS - 1, h], F32)
    for b in range(N_BUCKETS - 1):
        val = jnp.where(bucket == b, tab_ref[b, h], val)
    o_ref[...] = jnp.where(dist < 0, NEG_INF, val)


def _prompt_bias_tiles(rel_bias):
    assert MOBA_BLOCK + 1 >= FAR_DISTANCE
    return pl.pallas_call(
        _prompt_bias_kernel,
        grid=(N_HEADS, 2),
        in_specs=[pl.BlockSpec(memory_space=pltpu.SMEM)],
        out_specs=pl.BlockSpec((None, None, MOBA_BLOCK, MOBA_BLOCK), lambda h, c: (h, c, 0, 0)),
        out_shape=jax.ShapeDtypeStruct((N_HEADS, 2, MOBA_BLOCK, MOBA_BLOCK), F32),
        compiler_params=pltpu.CompilerParams(dimension_semantics=("arbitrary", "arbitrary")),
        name="prompt_bias_tiles",
    )(rel_bias)


def _sample_bias_kernel(tab_ref, o_ref, *, t_new):
    shape = o_ref.shape
    row = lax.broadcasted_iota(jnp.int32, shape, 0)
    lane = lax.broadcasted_iota(jnp.int32, shape, 1)
    t = row % t_new
    dist = jnp.where(lane < MOBA_BLOCK, MOBA_BLOCK + t - lane,
                     jnp.where(lane < MOBA_BLOCK + 128, t - (lane - MOBA_BLOCK), FAR_DISTANCE))
    bucket = _rel_bucket(dist)
    tab = tab_ref[...]
    val = jnp.broadcast_to(tab[:, N_BUCKETS - 1:N_BUCKETS], shape)
    for b in range(N_BUCKETS - 1):
        val = jnp.where(bucket == b, tab[:, b:b + 1], val)
    o_ref[...] = jnp.where(dist < 0, NEG_INF, val)


def _sample_bias(rel_bias, t_new):
    rows = N_HEADS * t_new
    tab_rows = jnp.repeat(rel_bias.T, t_new, axis=0)
    return pl.pallas_call(
        functools.partial(_sample_bias_kernel, t_new=t_new),
        out_shape=jax.ShapeDtypeStruct((rows, MOBA_BLOCK + 256), F32),
        name="sample_bias",
    )(tab_rows)


def _inproj_kernel(*refs, transposed_qkv):
    if transposed_qkv:
        (x_ref, g_ref, wqkv_ref, w_ref, lng_ref, lnb_ref, _, _,
         q_ref, k_ref, v_ref, ga_ref, u_ref, vn_ref, gb_ref, ma_ref, mb_ref) = refs
    else:
        (x_ref, g_ref, wqkv_ref, w_ref, lng_ref, lnb_ref,
         q_ref, k_ref, v_ref, ga_ref, u_ref, vn_ref, gb_ref, ma_ref, mb_ref) = refs
    x = x_ref[...]
    d_model = x.shape[-1]
    d_sgu = u_ref.shape[-1]
    h = (x * lax.rsqrt(jnp.mean(x * x, axis=-1, keepdims=True) + EPS) * g_ref[...]).astype(BF16)

    if transposed_qkv:
        qkv_t = lax.dot_general(wqkv_ref[...], h, NT_DIMS, preferred_element_type=F32)
        q_ref[...] = qkv_t[0:D_ATTN]
        k_ref[...] = qkv_t[D_ATTN:2 * D_ATTN]
        v_ref[...] = qkv_t[2 * D_ATTN:3 * D_ATTN]
    else:
        qkv = jnp.dot(h, wqkv_ref[...], preferred_element_type=F32)
        q_ref[...] = qkv[:, 0:D_ATTN]
        k_ref[...] = qkv[:, D_ATTN:2 * D_ATTN]
        v_ref[...] = qkv[:, 2 * D_ATTN:3 * D_ATTN]

    def seg(lo, width):
        return jnp.dot(h, w_ref[:, lo:lo + width], preferred_element_type=F32)

    ga_ref[...] = seg(0, D_ATTN)
    base = D_ATTN
    u_ref[...] = _gelu_tanh(seg(base, d_sgu))
    vs = _gelu_tanh(seg(base + d_sgu, d_sgu))
    mu = jnp.mean(vs, axis=-1, keepdims=True)
    var = jnp.mean(jnp.square(vs - mu), axis=-1, keepdims=True)
    vn_ref[...] = (vs - mu) * lax.rsqrt(var + LN_EPS) * lng_ref[...] + lnb_ref[...]
    gb_ref[...] = seg(base + 2 * d_sgu, d_sgu)
    ma_ref[...] = seg(base + 3 * d_sgu, d_model)
    mb_ref[...] = seg(base + 3 * d_sgu + d_model, d_model)


def _inproj(x, norm_g, wqkv_bf, w_bf, ln_g, ln_b, tm, kv_stack=None, layer=None, seq_len=None):
    n_tok, d_model = x.shape
    d_sgu = ln_g.shape[-1]
    assert n_tok % tm == 0
    widths = [D_ATTN] + [d_sgu] * 3 + [d_model] * 2
    assert sum(widths) == w_bf.shape[1]
    row = lambda i: (i, 0)
    fixed = lambda i: (0, 0)
    transposed = kv_stack is not None
    in_specs = [pl.BlockSpec((tm, d_model), row),
                pl.BlockSpec((1, d_model), fixed),
                pl.BlockSpec(wqkv_bf.shape, fixed),
                pl.BlockSpec(w_bf.shape, fixed),
                pl.BlockSpec((1, d_sgu), fixed),
                pl.BlockSpec((1, d_sgu), fixed)]
    args = [x, norm_g.reshape(1, d_model), wqkv_bf, w_bf, ln_g.reshape(1, d_sgu), ln_b.reshape(1, d_sgu)]
    rest_specs = [pl.BlockSpec((tm, w), row) for w in widths]
    rest_shapes = [jax.ShapeDtypeStruct((n_tok, w), F32) for w in widths]
    aliases = {}
    if transposed:
        assert seq_len % tm == 0 and n_tok % seq_len == 0
        tiles = seq_len // tm
        batch = n_tok // seq_len
        in_specs += [pl.BlockSpec(memory_space=pl.ANY)] * 2
        args += list(kv_stack)
        aliases = {6: 1, 7: 2}
        stack_spec = pl.BlockSpec((None, None, D_ATTN, tm), lambda i: (layer, i // tiles, 0, i % tiles))
        qkv_specs = [pl.BlockSpec((None, D_ATTN, tm), lambda i: (i // tiles, 0, i % tiles)), stack_spec, stack_spec]
        qkv_shapes = [jax.ShapeDtypeStruct((batch, D_ATTN, seq_len), F32),
                      jax.ShapeDtypeStruct(kv_stack[0].shape, F32), jax.ShapeDtypeStruct(kv_stack[1].shape, F32)]
    else:
        qkv_specs = [pl.BlockSpec((tm, D_ATTN), row)] * 3
        qkv_shapes = [jax.ShapeDtypeStruct((n_tok, D_ATTN), F32)] * 3
    return pl.pallas_call(
        functools.partial(_inproj_kernel, transposed_qkv=transposed),
        grid=(n_tok // tm,),
        in_specs=in_specs,
        out_specs=qkv_specs + rest_specs,
        out_shape=qkv_shapes + rest_shapes,
        input_output_aliases=aliases,
        compiler_params=pltpu.CompilerParams(dimension_semantics=("arbitrary",),
                                             vmem_limit_bytes=VMEM_LIMIT_BYTES),
        name="inproj",
    )(*args)


def _moba_prompt_kernel(tab_ref, qt_ref, kt_ref, vt_ref, bias_ref, o_ref,
                        kb_scr, vb_scr, kmbd_scr, neg_scr, qh_scr, s_scr, m_scr, l_scr, acc_scr):
    qt = pl.program_id(1)
    n_blk = kb_scr.shape[0]
    tq = qt_ref.shape[1]

    @pl.when(qt == 0)
    def _():
        lane_head = lax.broadcasted_iota(jnp.int32, (1, D_ATTN), 1) // HEAD_DIM
        for n in range(n_blk):
            kblk = kt_ref[:, n * MOBA_BLOCK:(n + 1) * MOBA_BLOCK].T
            kb_scr[n] = kblk.astype(BF16)
            vb_scr[n] = vt_ref[:, n * MOBA_BLOCK:(n + 1) * MOBA_BLOCK].astype(BF16)
            k_mean = jnp.sum(kblk, axis=0, keepdims=True) * (1.0 / MOBA_BLOCK)
            for h in range(N_HEADS):
                kmbd_scr[h * n_blk + n:h * n_blk + n + 1, :] = jnp.where(lane_head == h, k_mean, 0.0)

    q_t = (qt_ref[...] * SCALE).astype(BF16)

    gate = jnp.dot(kmbd_scr[...].astype(BF16), q_t, preferred_element_type=F32)
    blk = lax.broadcasted_iota(jnp.int32, (n_blk, tq), 0)
    half = lax.broadcasted_iota(jnp.int32, (2 * HEAD_DIM, tq), 0) // HEAD_DIM
    for h in range(N_HEADS):
        g = gate[h * n_blk:(h + 1) * n_blk, :]
        beaten = jnp.zeros((n_blk, tq), jnp.int32)
        for m in range(n_blk):
            gm = g[m:m + 1, :]
            wins = (gm > g) | ((gm == g) & (m < blk))
            beaten = beaten + jnp.where(wins, 1, 0) * (m < qt).astype(jnp.int32)
        selected = (blk < qt) & (beaten < MOBA_TOPK)
        far = jnp.where(blk < qt - 1, tab_ref[N_BUCKETS - 1, h], 0.0)
        neg_scr[h] = jnp.where(selected, far, NEG_INF)
        pair = h // 2
        q_pair = q_t[pair * 2 * HEAD_DIM:(pair + 1) * 2 * HEAD_DIM, :]
        qh_scr[h] = jnp.where(half == h % 2, q_pair, jnp.zeros_like(q_pair))

    def scores(h, n):
        pair = h // 2
        k_pair = kb_scr[n, :, pair * 2 * HEAD_DIM:(pair + 1) * 2 * HEAD_DIM]
        return jnp.dot(k_pair, qh_scr[h], preferred_element_type=F32)

    def sublane_groups(a):
        return a.reshape(a.shape[0] // 8, 8, a.shape[1])

    n_group = s_scr.shape[0]
    for g in range(N_HEADS // n_group):
        heads = range(g * n_group, (g + 1) * n_group)

        def score_block(h, n, extra):
            s = (scores(h, n) + extra) * LOG2_E
            s_scr[h % n_group, n] = s
            return jnp.max(sublane_groups(s), axis=0)

        for h in heads:
            m_scr[h] = score_block(h, qt, bias_ref[h, 0])

        @pl.when(qt >= 1)
        def _():
            n = qt - 1
            for h in heads:
                m_scr[h] = jnp.maximum(m_scr[h], score_block(h, n, bias_ref[h, 1] + neg_scr[h, pl.ds(n, 1), :]))

        def far_body(n, carry):
            for h in heads:
                m_scr[h] = jnp.maximum(m_scr[h], score_block(h, n, neg_scr[h, pl.ds(n, 1), :]))
            return carry

        lax.fori_loop(0, jnp.maximum(qt - 1, 0), far_body, 0)

        m_fin = {h: jnp.max(m_scr[h], axis=0, keepdims=True) for h in heads}
        for h in heads:
            l_scr[h] = jnp.zeros(l_scr.shape[1:], F32)
            acc_scr[h * HEAD_DIM:(h + 1) * HEAD_DIM, :] = jnp.zeros((HEAD_DIM, tq), F32)

        def weight_body(n, carry):
            for h in heads:
                p = jnp.exp2(s_scr[h % n_group, n] - m_fin[h])
                l_scr[h] = l_scr[h] + jnp.sum(sublane_groups(p), axis=0)
                v_t = vb_scr[n, h * HEAD_DIM:(h + 1) * HEAD_DIM, :]
                rows = slice(h * HEAD_DIM, (h + 1) * HEAD_DIM)
                acc_scr[rows, :] = acc_scr[rows, :] + jnp.dot(v_t, p.astype(BF16), preferred_element_type=F32)
            return carry

        lax.fori_loop(0, qt + 1, weight_body, 0)

        for h in heads:
            rows = slice(h * HEAD_DIM, (h + 1) * HEAD_DIM)
            acc_scr[rows, :] = acc_scr[rows, :] / jnp.sum(l_scr[h], axis=0, keepdims=True)
    o_ref[...] = acc_scr[...].T


PROMPT_HEAD_GROUP = 8


def _moba_prompt(rel_bias, q_t, k_all, v_all, layer, bias_tiles):
    b, _, s = q_t.shape
    assert s % MOBA_BLOCK == 0
    n_blk = s // MOBA_BLOCK
    tq = MOBA_BLOCK
    stack_spec = pl.BlockSpec((None, None, D_ATTN, s), lambda i, j: (layer, i, 0, 0))
    return pl.pallas_call(
        _moba_prompt_kernel,
        grid=(b, s // tq),
        in_specs=[pl.BlockSpec(memory_space=pltpu.SMEM),
                  pl.BlockSpec((None, D_ATTN, tq), lambda i, j: (i, 0, j)),
                  stack_spec, stack_spec,
                  pl.BlockSpec(bias_tiles.shape, lambda i, j: (0, 0, 0, 0))],
        out_specs=pl.BlockSpec((None, tq, D_ATTN), lambda i, j: (i, j, 0)),
        out_shape=jax.ShapeDtypeStruct((b, s, D_ATTN), F32),
        scratch_shapes=[pltpu.VMEM((n_blk, MOBA_BLOCK, D_ATTN), BF16),
                        pltpu.VMEM((n_blk, D_ATTN, MOBA_BLOCK), BF16),
                        pltpu.VMEM((N_HEADS * n_blk, D_ATTN), F32),
                        pltpu.VMEM((N_HEADS, n_blk, tq), F32),
                        pltpu.VMEM((N_HEADS, 2 * HEAD_DIM, tq), BF16),
                        pltpu.VMEM((PROMPT_HEAD_GROUP, n_blk, MOBA_BLOCK, tq), F32),
                        pltpu.VMEM((N_HEADS, 8, tq), F32),
                        pltpu.VMEM((N_HEADS, 8, tq), F32),
                        pltpu.VMEM((D_ATTN, tq), F32)],
        compiler_params=pltpu.CompilerParams(dimension_semantics=("arbitrary", "arbitrary"),
                                             vmem_limit_bytes=VMEM_LIMIT_BYTES),
        name="moba_prompt",
    )(rel_bias, q_t, k_all, v_all, bias_tiles)


def _moba_sample_kernel(pt_ref, q_ref, kn_ref, vn_ref, sb_ref, *refs, blocks_per_step, pages_per_block, n_blk):
    n_pages = blocks_per_step * pages_per_block
    k_pages = refs[:n_pages]
    v_pages = refs[n_pages:2 * n_pages]
    o_ref = refs[2 * n_pages]
    g_scr, m_scr, l_scr, o_scr = refs[2 * n_pages + 1:]
    del pt_ref
    step = pl.program_id(1)
    t_new = kn_ref.shape[0]
    rows = q_ref.shape[0]
    page = k_pages[0].shape[-1]

    row_head = lax.broadcasted_iota(jnp.int32, (rows, D_ATTN), 0) // t_new
    lane_head = lax.broadcasted_iota(jnp.int32, (rows, D_ATTN), 1) // HEAD_DIM
    own_lanes = row_head == lane_head
    q_rows = jnp.where(own_lanes, q_ref[...] * SCALE, 0.0)
    q_bf = q_rows.astype(BF16)
    lane = lax.broadcasted_iota(jnp.int32, (rows, 128), 1)

    @pl.when(step == 0)
    def _():
        m_scr[...] = jnp.full(m_scr.shape, NEG_INF, F32)

    def block_t(pages, jb):
        parts = [pages[jb * pages_per_block + jp][...].reshape(D_ATTN, page) for jp in range(pages_per_block)]
        return jnp.concatenate(parts, axis=1).astype(BF16)

    def fold_pages(a):
        return sum(a[:, jp * page:(jp + 1) * page] for jp in range(pages_per_block))

    far_bias = sb_ref[:, MOBA_BLOCK + 128:MOBA_BLOCK + 129]
    m_all = m_scr[...]
    blocks = [step * blocks_per_step + jb for jb in range(blocks_per_step)]
    scores = [jnp.dot(q_bf, block_t(k_pages, jb), preferred_element_type=F32)
              for jb in range(blocks_per_step)]
    probs = []
    for n, s in zip(blocks, scores):
        g_scr[n] = fold_pages(s)
        newest = (jnp.zeros(s.shape, jnp.int32) + n) == n_blk - 1
        s = s + jnp.where(newest, sb_ref[:, 0:MOBA_BLOCK], far_bias)
        m_blk = jnp.max(s, axis=1, keepdims=True)
        p = jnp.exp(s - m_blk)
        l_scr[n] = fold_pages(p)
        m_all = jnp.where(lane == n, m_blk, m_all)
        probs.append(p.astype(BF16))
    m_scr[...] = m_all
    for jb, (n, p) in enumerate(zip(blocks, probs)):
        o_scr[n] = lax.dot_general(p, block_t(v_pages, jb), NT_DIMS, preferred_element_type=F32)

    @pl.when(step == pl.num_programs(1) - 1)
    def _():
        g = jnp.full((rows, 128), NEG_INF, F32)
        l_all = jnp.zeros((rows, 128), F32)
        for n in range(n_blk):
            g = jnp.where(lane == n, jnp.sum(g_scr[n], axis=1, keepdims=True), g)
            l_all = jnp.where(lane == n, jnp.sum(l_scr[n], axis=1, keepdims=True), l_all)
        beaten = jnp.zeros(g.shape, jnp.int32)
        for m in range(n_blk):
            gm = g[:, m:m + 1]
            beaten = beaten + jnp.where((gm > g) | ((gm == g) & (m < lane)), 1, 0)
        selected = (beaten < MOBA_TOPK) & (lane < n_blk)
        m_sel = jnp.where(selected, m_scr[...], NEG_INF)

        s_loc = jnp.full((rows, 128), NEG_INF, F32)
        for t in range(t_new):
            col = jnp.sum(q_rows * kn_ref[t:t + 1, :], axis=1, keepdims=True)
            s_loc = jnp.where(lane == t, col, s_loc)
        s_loc = s_loc + sb_ref[:, MOBA_BLOCK:MOBA_BLOCK + 128]

        m_tot = jnp.maximum(jnp.max(m_sel, axis=1, keepdims=True), jnp.max(s_loc, axis=1, keepdims=True))
        w_blk = jnp.where(selected, jnp.exp(m_sel - m_tot), 0.0)
        p_loc = jnp.exp(s_loc - m_tot)
        denom = jnp.sum(w_blk * l_all, axis=1, keepdims=True) + jnp.sum(p_loc, axis=1, keepdims=True)
        acc = jnp.zeros((rows, D_ATTN), F32)
        for n in range(n_blk):
            acc = acc + w_blk[:, n:n + 1] * o_scr[n]
        for t in range(t_new):
            acc = acc + p_loc[:, t:t + 1] * vn_ref[t:t + 1, :]
        acc = jnp.where(own_lanes, acc / denom, 0.0)
        folded = acc
        for h in range(1, N_HEADS):
            folded = folded + pltpu.roll(acc, rows - h * t_new, 0)
        o_ref[...] = folded[:t_new, :]


def _moba_sample(q, k_new, v_new, cache_kt, cache_vt, layer, page_table, sbias, blocks_per_step):
    db, t_new, _ = q.shape
    page = cache_kt.shape[-1]
    n_pages = page_table.shape[1]
    pages_per_block = MOBA_BLOCK // page
    assert page == 128 and MOBA_BLOCK % page == 0
    assert (n_pages * page) % MOBA_BLOCK == 0 and t_new <= 128
    n_blk = n_pages // pages_per_block
    assert MOBA_TOPK <= n_blk <= 128 and n_blk % blocks_per_step == 0
    assert MOBA_BLOCK + 1 >= FAR_DISTANCE
    rows = N_HEADS * t_new
    q_rows = jnp.tile(q, (1, N_HEADS, 1))
    pages_per_step = blocks_per_step * pages_per_block

    def page_spec(j):
        return pl.BlockSpec((None, None, N_HEADS, HEAD_DIM, page),
                            lambda b, s, pt, j=j: (layer, pt[b, s * pages_per_step + j], 0, 0, 0))

    tok_spec = pl.BlockSpec((None, t_new, D_ATTN), lambda b, s, pt: (b, 0, 0))
    grid_spec = pltpu.PrefetchScalarGridSpec(
        num_scalar_prefetch=1,
        grid=(db, n_blk // blocks_per_step),
        in_specs=[pl.BlockSpec((None, rows, D_ATTN), lambda b, s, pt: (b, 0, 0)), tok_spec, tok_spec,
                  pl.BlockSpec(sbias.shape, lambda b, s, pt: (0, 0))]
                 + [page_spec(j) for j in range(pages_per_step)] * 2,
        out_specs=tok_spec,
        scratch_shapes=[pltpu.VMEM((n_blk, rows, page), F32), pltpu.VMEM((rows, 128), F32),
                        pltpu.VMEM((n_blk, rows, page), F32), pltpu.VMEM((n_blk, rows, D_ATTN), F32)],
    )
    return pl.pallas_call(
        functools.partial(_moba_sample_kernel, blocks_per_step=blocks_per_step,
                          pages_per_block=pages_per_block, n_blk=n_blk),
        grid_spec=grid_spec,
        out_shape=jax.ShapeDtypeStruct((db, t_new, D_ATTN), F32),
        compiler_params=pltpu.CompilerParams(dimension_semantics=("arbitrary", "arbitrary"),
                                             vmem_limit_bytes=VMEM_LIMIT_BYTES),
        name="moba_sample",
    )(page_table, q_rows, k_new, v_new, sbias, *([cache_kt] * pages_per_step), *([cache_vt] * pages_per_step))


def _merge_kernel(x_ref, attn_ref, ga_ref, u_ref, vn_ref, gb_ref, ma_ref, mb_ref,
                  mix_ref, sb_ref, wpa_ref, wpb_ref, wo_ref, fg_ref, o_ref, *, final):
    tm = x_ref.shape[0]
    d_sgu = u_ref.shape[-1]
    group_dim = d_sgu // N_GROUPS
    first_of_pair = lax.broadcasted_iota(jnp.int32, (CHUNK, 2 * group_dim), 1) < group_dim
    z_chunks = []
    for c in range(tm // CHUNK):
        vn = vn_ref[c * CHUNK:(c + 1) * CHUNK, :].astype(BF16)
        z_pairs = []
        for j in range(N_GROUPS // 2):
            v_pair = vn[:, j * 2 * group_dim:(j + 1) * 2 * group_dim]
            z_a = jnp.dot(mix_ref[2 * j], v_pair, preferred_element_type=F32)
            z_b = jnp.dot(mix_ref[2 * j + 1], v_pair, preferred_element_type=F32)
            z_pairs.append(jnp.where(first_of_pair, z_a, z_b))
        z_chunks.append(jnp.concatenate(z_pairs, axis=1) + sb_ref[...])
    z = jnp.concatenate(z_chunks, axis=0) if len(z_chunks) > 1 else z_chunks[0]
    y_b = (u_ref[...] * z) * _silu(gb_ref[...])
    y_a = attn_ref[...] * _silu(ga_ref[...])
    p_a = jnp.dot(y_a.astype(BF16), wpa_ref[...], preferred_element_type=F32)
    p_b = jnp.dot(y_b.astype(BF16), wpb_ref[...], preferred_element_type=F32)
    mixed = jax.nn.sigmoid(ma_ref[...]) * p_a + jax.nn.sigmoid(mb_ref[...]) * p_b
    out = x_ref[...] + jnp.dot(mixed.astype(BF16), wo_ref[...], preferred_element_type=F32)
    if final:
        out = out * lax.rsqrt(jnp.mean(out * out, axis=-1, keepdims=True) + EPS) * fg_ref[...]
    o_ref[...] = out


def _merge(x, attn, ga, u, vn, gb, ma, mb, mix_bf, sgu_bias, wpa_bf, wpb_bf, wo_bf, final_g, tm, final):
    n_tok, d_model = x.shape
    d_sgu = u.shape[-1]
    assert n_tok % tm == 0 and tm % CHUNK == 0
    row = lambda i: (i, 0)
    fixed2 = lambda i: (0, 0)
    narrow = pl.BlockSpec((tm, d_sgu), row)
    wide = pl.BlockSpec((tm, d_model), row)
    return pl.pallas_call(
        functools.partial(_merge_kernel, final=final),
        grid=(n_tok // tm,),
        in_specs=[wide, pl.BlockSpec((tm, D_ATTN), row), pl.BlockSpec((tm, D_ATTN), row),
                  narrow, narrow, narrow, wide, wide,
                  pl.BlockSpec(mix_bf.shape, lambda i: (0, 0, 0)),
                  pl.BlockSpec(sgu_bias.shape, fixed2),
                  pl.BlockSpec(wpa_bf.shape, fixed2),
                  pl.BlockSpec(wpb_bf.shape, fixed2),
                  pl.BlockSpec(wo_bf.shape, fixed2),
                  pl.BlockSpec((1, d_model), fixed2)],
        out_specs=wide,
        out_shape=jax.ShapeDtypeStruct((n_tok, d_model), F32),
        compiler_params=pltpu.CompilerParams(dimension_semantics=("arbitrary",),
                                             vmem_limit_bytes=VMEM_LIMIT_BYTES),
        name="merge",
    )(x, attn, ga, u, vn, gb, ma, mb, mix_bf, sgu_bias, wpa_bf, wpb_bf, wo_bf, final_g.reshape(1, d_model))


PROMPT_TOKEN_TILE = 256
SAMPLE_BLOCKS_PER_STEP = 8


def kernel(x_prompt, x_sample, cache_k, cache_v, page_table, rel_bias, norm_g, w_in, sgu_ln_g, sgu_ln_b,
           w_s, b_s, w_pa, w_pb, w_o, final_norm_g):
    b, s, d_model = x_prompt.shape
    db, t_new, _ = x_sample.shape
    depth = w_in.shape[0]
    d_sgu = sgu_ln_g.shape[-1]
    group_dim = d_sgu // N_GROUPS
    assert s % CHUNK == 0 and (db * t_new) % CHUNK == 0 and CHUNK % t_new == 0
    assert cache_k.shape[3] == N_HEADS and cache_k.shape[4] == HEAD_DIM

    bias_tiles = _prompt_bias_tiles(rel_bias)
    sbias = _sample_bias(rel_bias, t_new)
    cache_kt = jnp.transpose(cache_k, (0, 1, 3, 4, 2))
    cache_vt = jnp.transpose(cache_v, (0, 1, 3, 4, 2))

    xp = x_prompt.reshape(b * s, d_model)
    xs = x_sample.reshape(db * t_new, d_model)
    k_all = jnp.zeros((depth, b, D_ATTN, s), F32)
    v_all = jnp.zeros((depth, b, D_ATTN, s), F32)
    k_s, v_s, sgu_rows = [], [], []
    for l in range(depth):
        final = l == depth - 1
        wqkv = w_in[l][:, :3 * D_ATTN]
        wqkv_bf, wqkv_t_bf = wqkv.astype(BF16), wqkv.T.astype(BF16)
        w_bf = w_in[l][:, 3 * D_ATTN:].astype(BF16)
        wpa_bf, wpb_bf, wo_bf = w_pa[l].astype(BF16), w_pb[l].astype(BF16), w_o[l].astype(BF16)
        mix_p = jnp.tril(w_s[l]).astype(BF16)
        bias_p = jnp.repeat(b_s[l].T, group_dim, axis=1)
        corner = jnp.tril(w_s[l][:, :t_new, :t_new])
        eye = jnp.eye(CHUNK // t_new, dtype=F32)
        mix_s = jnp.einsum("ab,gts->gatbs", eye, corner).reshape(N_GROUPS, CHUNK, CHUNK).astype(BF16)
        bias_s = jnp.tile(jnp.repeat(b_s[l][:, :t_new].T, group_dim, axis=1), (CHUNK // t_new, 1))

        q_t, k_all, v_all, ga, u, vn, gb, ma, mb = _inproj(
            xp, norm_g[l], wqkv_t_bf, w_bf, sgu_ln_g[l], sgu_ln_b[l], PROMPT_TOKEN_TILE,
            kv_stack=(k_all, v_all), layer=l, seq_len=s)
        attn = _moba_prompt(rel_bias, q_t, k_all, v_all, l, bias_tiles)
        xp = _merge(xp, attn.reshape(b * s, D_ATTN), ga, u, vn, gb, ma, mb, mix_p, bias_p,
                    wpa_bf, wpb_bf, wo_bf, final_norm_g, PROMPT_TOKEN_TILE, final)

        q, k, v, ga, u, vn, gb, ma, mb = _inproj(xs, norm_g[l], wqkv_bf, w_bf, sgu_ln_g[l], sgu_ln_b[l], CHUNK)
        attn = _moba_sample(q.reshape(db, t_new, D_ATTN), k.reshape(db, t_new, D_ATTN), v.reshape(db, t_new, D_ATTN),
                            cache_kt, cache_vt, l, page_table, sbias, SAMPLE_BLOCKS_PER_STEP)
        xs = _merge(xs, attn.reshape(db * t_new, D_ATTN), ga, u, vn, gb, ma, mb, mix_s, bias_s,
                    wpa_bf, wpb_bf, wo_bf, final_norm_g, CHUNK, final)
        k_s.append(k.reshape(db, t_new, N_HEADS, HEAD_DIM))
        v_s.append(v.reshape(db, t_new, N_HEADS, HEAD_DIM))
        sgu_rows.append(vn.reshape(db, t_new, d_sgu))

    k_prompt = jnp.transpose(k_all.reshape(depth, b, N_HEADS, HEAD_DIM, s), (0, 1, 4, 2, 3))
    v_prompt = jnp.transpose(v_all.reshape(depth, b, N_HEADS, HEAD_DIM, s), (0, 1, 4, 2, 3))
    return (xp.reshape(b, s, d_model), xs.reshape(db, t_new, d_model),
            k_prompt, v_prompt, jnp.stack(k_s), jnp.stack(v_s), jnp.stack(sgu_rows))
```

```python
import functools
import math

import numpy as np
import jax
import jax.numpy as jnp
from jax import lax
from jax.experimental import pallas as pl
from jax.experimental.pallas import tpu as pltpu

N_HEADS = 8
HEAD_DIM = 64
D_ATTN = N_HEADS * HEAD_DIM
MOBA_BLOCK = 256
MOBA_TOPK = 3
N_GROUPS = 8
CHUNK = 128
N_BUCKETS = 32
MAX_DISTANCE = 128
EPS = 1e-6
LN_EPS = 1e-5
SCALE = HEAD_DIM ** -0.5
LOG2_E = math.log2(math.e)

F32 = jnp.float32
BF16 = jnp.bfloat16
NEG_INF = float("-inf")
NT_DIMS = (((1,), (1,)), ((), ()))

VMEM_LIMIT_BYTES = 56 * 1024 * 1024


def _bucket_thresholds():
    max_exact = N_BUCKETS // 2
    d = np.arange(max_exact, 4 * MAX_DISTANCE, dtype=np.float64)
    log_b = max_exact + (np.log(d / max_exact) / math.log(MAX_DISTANCE / max_exact) * (N_BUCKETS - max_exact)).astype(np.int64)
    bucket = np.minimum(log_b, N_BUCKETS - 1)
    return [int(d[np.argmax(bucket >= b)]) for b in range(max_exact + 1, N_BUCKETS)]


BUCKET_THRESHOLDS = _bucket_thresholds()
FAR_DISTANCE = BUCKET_THRESHOLDS[-1]


def _rel_bucket(dist):
    max_exact = N_BUCKETS // 2
    log_b = jnp.full(dist.shape, max_exact, jnp.int32)
    for thr in BUCKET_THRESHOLDS:
        log_b = log_b + jnp.where(dist >= thr, 1, 0)
    return jnp.where(dist < max_exact, dist, log_b)


def _gelu_tanh(x):
    return 0.5 * x * (1.0 + jnp.tanh(math.sqrt(2.0 / math.pi) * (x + 0.044715 * (x * x * x))))


def _silu(x):
    return x * jax.nn.sigmoid(x)


def _prompt_bias_kernel(tab_ref, o_ref):
    h = pl.program_id(0)
    c = pl.program_id(1)
    shape = o_ref.shape
    key = lax.broadcasted_iota(jnp.int32, shape, 0)
    qry = lax.broadcasted_iota(jnp.int32, shape, 1)
    dist = c * MOBA_BLOCK + qry - key
    bucket = _rel_bucket(dist)
    val = jnp.full(shape, tab_ref[N_BUCKETS - 1, h], F32)
    for b in range(N_BUCKETS - 1):
        val = jnp.where(bucket == b, tab_ref[b, h], val)
    o_ref[...] = jnp.where(dist < 0, NEG_INF, val)


def _prompt_bias_tiles(rel_bias):
    assert MOBA_BLOCK + 1 >= FAR_DISTANCE
    return pl.pallas_call(
        _prompt_bias_kernel,
        grid=(N_HEADS, 2),
        in_specs=[pl.BlockSpec(memory_space=pltpu.SMEM)],
        out_specs=pl.BlockSpec((None, None, MOBA_BLOCK, MOBA_BLOCK), lambda h, c: (h, c, 0, 0)),
        out_shape=jax.ShapeDtypeStruct((N_HEADS, 2, MOBA_BLOCK, MOBA_BLOCK), F32),
        compiler_params=pltpu.CompilerParams(dimension_semantics=("arbitrary", "arbitrary")),
        name="prompt_bias_tiles",
    )(rel_bias)


def _sample_bias_kernel(tab_ref, o_ref, *, t_new):
    shape = o_ref.shape
    row = lax.broadcasted_iota(jnp.int32, shape, 0)
    lane = lax.broadcasted_iota(jnp.int32, shape, 1)
    t = row % t_new
    dist = jnp.where(lane < MOBA_BLOCK, MOBA_BLOCK + t - lane,
                     jnp.where(lane < MOBA_BLOCK + 128, t - (lane - MOBA_BLOCK), FAR_DISTANCE))
    bucket = _rel_bucket(dist)
    tab = tab_ref[...]
    val = jnp.broadcast_to(tab[:, N_BUCKETS - 1:N_BUCKETS], shape)
    for b in range(N_BUCKETS - 1):
        val = jnp.where(bucket == b, tab[:, b:b + 1], val)
    o_ref[...] = jnp.where(dist < 0, NEG_INF, val)


def _sample_bias(rel_bias, t_new):
    rows = N_HEADS * t_new
    tab_rows = jnp.repeat(rel_bias.T, t_new, axis=0)
    return pl.pallas_call(
        functools.partial(_sample_bias_kernel, t_new=t_new),
        out_shape=jax.ShapeDtypeStruct((rows, MOBA_BLOCK + 256), F32),
        name="sample_bias",
    )(tab_rows)


def _inproj_kernel(*refs, transposed_qkv):
    if transposed_qkv:
        (x_ref, g_ref, wqkv_ref, w_ref, lng_ref, lnb_ref, _, _,
         q_ref, k_ref, v_ref, ga_ref, u_ref, vn_ref, gb_ref, ma_ref, mb_ref) = refs
    else:
        (x_ref, g_ref, wqkv_ref, w_ref, lng_ref, lnb_ref,
         q_ref, k_ref, v_ref, ga_ref, u_ref, vn_ref, gb_ref, ma_ref, mb_ref) = refs
    x = x_ref[...]
    d_model = x.shape[-1]
    d_sgu = u_ref.shape[-1]
    h = (x * lax.rsqrt(jnp.mean(x * x, axis=-1, keepdims=True) + EPS) * g_ref[...]).astype(BF16)

    if transposed_qkv:
        qkv_t = lax.dot_general(wqkv_ref[...], h, NT_DIMS, preferred_element_type=F32)
        q_ref[...] = qkv_t[0:D_ATTN].astype(q_ref.dtype)
        k_ref[...] = qkv_t[D_ATTN:2 * D_ATTN]
        v_ref[...] = qkv_t[2 * D_ATTN:3 * D_ATTN]
    else:
        qkv = jnp.dot(h, wqkv_ref[...], preferred_element_type=F32)
        q_ref[...] = qkv[:, 0:D_ATTN]
        k_ref[...] = qkv[:, D_ATTN:2 * D_ATTN]
        v_ref[...] = qkv[:, 2 * D_ATTN:3 * D_ATTN]

    def seg(lo, width):
        return jnp.dot(h, w_ref[:, lo:lo + width], preferred_element_type=F32)

    act = ga_ref.dtype
    ga_ref[...] = seg(0, D_ATTN).astype(act)
    base = D_ATTN
    u_ref[...] = _gelu_tanh(seg(base, d_sgu)).astype(act)
    vs = _gelu_tanh(seg(base + d_sgu, d_sgu))
    mu = jnp.mean(vs, axis=-1, keepdims=True)
    var = jnp.mean(jnp.square(vs - mu), axis=-1, keepdims=True)
    vn_ref[...] = ((vs - mu) * lax.rsqrt(var + LN_EPS) * lng_ref[...] + lnb_ref[...]).astype(act)
    gb_ref[...] = seg(base + 2 * d_sgu, d_sgu).astype(act)
    ma_ref[...] = seg(base + 3 * d_sgu, d_model).astype(act)
    mb_ref[...] = seg(base + 3 * d_sgu + d_model, d_model).astype(act)


def _inproj(x, norm_g, wqkv_bf, w_bf, ln_g, ln_b, tm, kv_stack=None, layer=None, seq_len=None):
    n_tok, d_model = x.shape
    d_sgu = ln_g.shape[-1]
    assert n_tok % tm == 0
    widths = [D_ATTN] + [d_sgu] * 3 + [d_model] * 2
    assert sum(widths) == w_bf.shape[1]
    row = lambda i: (i, 0)
    fixed = lambda i: (0, 0)
    transposed = kv_stack is not None
    in_specs = [pl.BlockSpec((tm, d_model), row),
                pl.BlockSpec((1, d_model), fixed),
                pl.BlockSpec(wqkv_bf.shape, fixed),
                pl.BlockSpec(w_bf.shape, fixed),
                pl.BlockSpec((1, d_sgu), fixed),
                pl.BlockSpec((1, d_sgu), fixed)]
    args = [x, norm_g.reshape(1, d_model), wqkv_bf, w_bf, ln_g.reshape(1, d_sgu), ln_b.reshape(1, d_sgu)]
    rest_specs = [pl.BlockSpec((tm, w), row) for w in widths]
    act_dtype = BF16 if transposed else F32
    rest_shapes = [jax.ShapeDtypeStruct((n_tok, w), act_dtype) for w in widths]
    aliases = {}
    if transposed:
        assert seq_len % tm == 0 and n_tok % seq_len == 0
        tiles = seq_len // tm
        batch = n_tok // seq_len
        in_specs += [pl.BlockSpec(memory_space=pl.ANY)] * 2
        args += list(kv_stack)
        aliases = {6: 1, 7: 2}
        stack_spec = pl.BlockSpec((None, None, D_ATTN, tm), lambda i: (layer, i // tiles, 0, i % tiles))
        qkv_specs = [pl.BlockSpec((None, D_ATTN, tm), lambda i: (i // tiles, 0, i % tiles)), stack_spec, stack_spec]
        qkv_shapes = [jax.ShapeDtypeStruct((batch, D_ATTN, seq_len), BF16),
                      jax.ShapeDtypeStruct(kv_stack[0].shape, F32), jax.ShapeDtypeStruct(kv_stack[1].shape, F32)]
    else:
        qkv_specs = [pl.BlockSpec((tm, D_ATTN), row)] * 3
        qkv_shapes = [jax.ShapeDtypeStruct((n_tok, D_ATTN), F32)] * 3
    return pl.pallas_call(
        functools.partial(_inproj_kernel, transposed_qkv=transposed),
        grid=(n_tok // tm,),
        in_specs=in_specs,
        out_specs=qkv_specs + rest_specs,
        out_shape=qkv_shapes + rest_shapes,
        input_output_aliases=aliases,
        compiler_params=pltpu.CompilerParams(dimension_semantics=("arbitrary",),
                                             vmem_limit_bytes=VMEM_LIMIT_BYTES),
        name="inproj",
    )(*args)


def _moba_prompt_kernel(tab_ref, qt_ref, kt_ref, vt_ref, bias_ref, o_ref,
                        kb_scr, vb_scr, kmbd_scr, neg_scr, qh_scr, s_scr, m_scr, l_scr, acc_scr):
    qt = pl.program_id(1)
    n_blk = kb_scr.shape[0]
    tq = qt_ref.shape[1]

    @pl.when(qt == 0)
    def _():
        lane_head = lax.broadcasted_iota(jnp.int32, (1, D_ATTN), 1) // HEAD_DIM
        for n in range(n_blk):
            kblk = kt_ref[:, n * MOBA_BLOCK:(n + 1) * MOBA_BLOCK].T
            kb_scr[n] = kblk.astype(BF16)
            vb_scr[n] = vt_ref[:, n * MOBA_BLOCK:(n + 1) * MOBA_BLOCK].astype(BF16)
            k_mean = jnp.sum(kblk, axis=0, keepdims=True) * (1.0 / MOBA_BLOCK)
            for h in range(N_HEADS):
                kmbd_scr[h * n_blk + n:h * n_blk + n + 1, :] = jnp.where(lane_head == h, k_mean, 0.0)

    q_t = (qt_ref[...].astype(F32) * SCALE).astype(BF16)

    gate = jnp.dot(kmbd_scr[...].astype(BF16), q_t, preferred_element_type=F32)
    blk = lax.broadcasted_iota(jnp.int32, (n_blk, tq), 0)
    half = lax.broadcasted_iota(jnp.int32, (2 * HEAD_DIM, tq), 0) // HEAD_DIM
    for h in range(N_HEADS):
        g = gate[h * n_blk:(h + 1) * n_blk, :]
        beaten = jnp.zeros((n_blk, tq), jnp.int32)
        for m in range(n_blk):
            gm = g[m:m + 1, :]
            wins = (gm > g) | ((gm == g) & (m < blk))
            beaten = beaten + jnp.where(wins, 1, 0) * (m < qt).astype(jnp.int32)
        selected = (blk < qt) & (beaten < MOBA_TOPK)
        far = jnp.where(blk < qt - 1, tab_ref[N_BUCKETS - 1, h], 0.0)
        neg_scr[h] = jnp.where(selected, far, NEG_INF)
        pair = h // 2
        q_pair = q_t[pair * 2 * HEAD_DIM:(pair + 1) * 2 * HEAD_DIM, :]
        qh_scr[h] = jnp.where(half == h % 2, q_pair, jnp.zeros_like(q_pair))

    def scores(h, n):
        pair = h // 2
        k_pair = kb_scr[n, :, pair * 2 * HEAD_DIM:(pair + 1) * 2 * HEAD_DIM]
        return jnp.dot(k_pair, qh_scr[h], preferred_element_type=F32)

    def sublane_groups(a):
        return a.reshape(a.shape[0] // 8, 8, a.shape[1])

    n_group = s_scr.shape[0]
    for g in range(N_HEADS // n_group):
        heads = range(g * n_group, (g + 1) * n_group)

        def score_block(h, n, extra):
            s = (scores(h, n) + extra) * LOG2_E
            s_scr[h % n_group, n] = s
            return jnp.max(sublane_groups(s), axis=0)

        for h in heads:
            m_scr[h] = score_block(h, qt, bias_ref[h, 0])

        @pl.when(qt >= 1)
        def _():
            n = qt - 1
            for h in heads:
                m_scr[h] = jnp.maximum(m_scr[h], score_block(h, n, bias_ref[h, 1] + neg_scr[h, pl.ds(n, 1), :]))

        def far_body(n, carry):
            for h in heads:
                m_scr[h] = jnp.maximum(m_scr[h], score_block(h, n, neg_scr[h, pl.ds(n, 1), :]))
            return carry

        lax.fori_loop(0, jnp.maximum(qt - 1, 0), far_body, 0)

        m_fin = {h: jnp.max(m_scr[h], axis=0, keepdims=True) for h in heads}
        for h in heads:
            l_scr[h] = jnp.zeros(l_scr.shape[1:], F32)
            acc_scr[h * HEAD_DIM:(h + 1) * HEAD_DIM, :] = jnp.zeros((HEAD_DIM, tq), F32)

        def weight_body(n, carry):
            for h in heads:
                p = jnp.exp2(s_scr[h % n_group, n] - m_fin[h])
                l_scr[h] = l_scr[h] + jnp.sum(sublane_groups(p), axis=0)
                v_t = vb_scr[n, h * HEAD_DIM:(h + 1) * HEAD_DIM, :]
                rows = slice(h * HEAD_DIM, (h + 1) * HEAD_DIM)
                acc_scr[rows, :] = acc_scr[rows, :] + jnp.dot(v_t, p.astype(BF16), preferred_element_type=F32)
            return carry

        lax.fori_loop(0, qt + 1, weight_body, 0)

        for h in heads:
            rows = slice(h * HEAD_DIM, (h + 1) * HEAD_DIM)
            acc_scr[rows, :] = acc_scr[rows, :] / jnp.sum(l_scr[h], axis=0, keepdims=True)
    o_ref[...] = acc_scr[...].T.astype(o_ref.dtype)


PROMPT_HEAD_GROUP = 8


def _moba_prompt(rel_bias, q_t, k_all, v_all, layer, bias_tiles):
    b, _, s = q_t.shape
    assert s % MOBA_BLOCK == 0
    n_blk = s // MOBA_BLOCK
    tq = MOBA_BLOCK
    stack_spec = pl.BlockSpec((None, None, D_ATTN, s), lambda i, j: (layer, i, 0, 0))
    return pl.pallas_call(
        _moba_prompt_kernel,
        grid=(b, s // tq),
        in_specs=[pl.BlockSpec(memory_space=pltpu.SMEM),
                  pl.BlockSpec((None, D_ATTN, tq), lambda i, j: (i, 0, j)),
                  stack_spec, stack_spec,
                  pl.BlockSpec(bias_tiles.shape, lambda i, j: (0, 0, 0, 0))],
        out_specs=pl.BlockSpec((None, tq, D_ATTN), lambda i, j: (i, j, 0)),
        out_shape=jax.ShapeDtypeStruct((b, s, D_ATTN), BF16),
        scratch_shapes=[pltpu.VMEM((n_blk, MOBA_BLOCK, D_ATTN), BF16),
                        pltpu.VMEM((n_blk, D_ATTN, MOBA_BLOCK), BF16),
                        pltpu.VMEM((N_HEADS * n_blk, D_ATTN), F32),
                        pltpu.VMEM((N_HEADS, n_blk, tq), F32),
                        pltpu.VMEM((N_HEADS, 2 * HEAD_DIM, tq), BF16),
                        pltpu.VMEM((PROMPT_HEAD_GROUP, n_blk, MOBA_BLOCK, tq), F32),
                        pltpu.VMEM((N_HEADS, 8, tq), F32),
                        pltpu.VMEM((N_HEADS, 8, tq), F32),
                        pltpu.VMEM((D_ATTN, tq), F32)],
        compiler_params=pltpu.CompilerParams(dimension_semantics=("arbitrary", "arbitrary"),
                                             vmem_limit_bytes=VMEM_LIMIT_BYTES),
        name="moba_prompt",
    )(rel_bias, q_t, k_all, v_all, bias_tiles)


def _moba_sample_kernel(pt_ref, q_ref, kn_ref, vn_ref, sb_ref, *refs, blocks_per_step, pages_per_block, n_blk):
    n_pages = blocks_per_step * pages_per_block
    k_pages = refs[:n_pages]
    v_pages = refs[n_pages:2 * n_pages]
    o_ref = refs[2 * n_pages]
    g_scr, m_scr, l_scr, o_scr = refs[2 * n_pages + 1:]
    del pt_ref
    step = pl.program_id(1)
    t_new = kn_ref.shape[0]
    rows = q_ref.shape[0]
    page = k_pages[0].shape[-1]

    row_head = lax.broadcasted_iota(jnp.int32, (rows, D_ATTN), 0) // t_new
    lane_head = lax.broadcasted_iota(jnp.int32, (rows, D_ATTN), 1) // HEAD_DIM
    own_lanes = row_head == lane_head
    q_rows = jnp.where(own_lanes, q_ref[...] * SCALE, 0.0)
    q_bf = q_rows.astype(BF16)
    lane = lax.broadcasted_iota(jnp.int32, (rows, 128), 1)

    @pl.when(step == 0)
    def _():
        m_scr[...] = jnp.full(m_scr.shape, NEG_INF, F32)

    def block_t(pages, jb):
        parts = [pages[jb * pages_per_block + jp][...].reshape(D_ATTN, page) for jp in range(pages_per_block)]
        return jnp.concatenate(parts, axis=1).astype(BF16)

    def fold_pages(a):
        return sum(a[:, jp * page:(jp + 1) * page] for jp in range(pages_per_block))

    far_bias = sb_ref[:, MOBA_BLOCK + 128:MOBA_BLOCK + 129]
    m_all = m_scr[...]
    blocks = [step * blocks_per_step + jb for jb in range(blocks_per_step)]
    scores = [jnp.dot(q_bf, block_t(k_pages, jb), preferred_element_type=F32)
              for jb in range(blocks_per_step)]
    probs = []
    for n, s in zip(blocks, scores):
        g_scr[n] = fold_pages(s)
        newest = (jnp.zeros(s.shape, jnp.int32) + n) == n_blk - 1
        s = s + jnp.where(newest, sb_ref[:, 0:MOBA_BLOCK], far_bias)
        m_blk = jnp.max(s, axis=1, keepdims=True)
        p = jnp.exp(s - m_blk)
        l_scr[n] = fold_pages(p)
        m_all = jnp.where(lane == n, m_blk, m_all)
        probs.append(p.astype(BF16))
    m_scr[...] = m_all
    for jb, (n, p) in enumerate(zip(blocks, probs)):
        o_scr[n] = lax.dot_general(p, block_t(v_pages, jb), NT_DIMS, preferred_element_type=F32)

    @pl.when(step == pl.num_programs(1) - 1)
    def _():
        g = jnp.full((rows, 128), NEG_INF, F32)
        l_all = jnp.zeros((rows, 128), F32)
        for n in range(n_blk):
            g = jnp.where(lane == n, jnp.sum(g_scr[n], axis=1, keepdims=True), g)
            l_all = jnp.where(lane == n, jnp.sum(l_scr[n], axis=1, keepdims=True), l_all)
        beaten = jnp.zeros(g.shape, jnp.int32)
        for m in range(n_blk):
            gm = g[:, m:m + 1]
            beaten = beaten + jnp.where((gm > g) | ((gm == g) & (m < lane)), 1, 0)
        selected = (beaten < MOBA_TOPK) & (lane < n_blk)
        m_sel = jnp.where(selected, m_scr[...], NEG_INF)

        s_loc = jnp.full((rows, 128), NEG_INF, F32)
        for t in range(t_new):
            col = jnp.sum(q_rows * kn_ref[t:t + 1, :], axis=1, keepdims=True)
            s_loc = jnp.where(lane == t, col, s_loc)
        s_loc = s_loc + sb_ref[:, MOBA_BLOCK:MOBA_BLOCK + 128]

        m_tot = jnp.maximum(jnp.max(m_sel, axis=1, keepdims=True), jnp.max(s_loc, axis=1, keepdims=True))
        w_blk = jnp.where(selected, jnp.exp(m_sel - m_tot), 0.0)
        p_loc = jnp.exp(s_loc - m_tot)
        denom = jnp.sum(w_blk * l_all, axis=1, keepdims=True) + jnp.sum(p_loc, axis=1, keepdims=True)
        acc = jnp.zeros((rows, D_ATTN), F32)
        for n in range(n_blk):
            acc = acc + w_blk[:, n:n + 1] * o_scr[n]
        for t in range(t_new):
            acc = acc + p_loc[:, t:t + 1] * vn_ref[t:t + 1, :]
        acc = jnp.where(own_lanes, acc / denom, 0.0)
        folded = acc
        for h in range(1, N_HEADS):
            folded = folded + pltpu.roll(acc, rows - h * t_new, 0)
        o_ref[...] = folded[:t_new, :]


def _moba_sample(q, k_new, v_new, cache_kt, cache_vt, layer, page_table, sbias, blocks_per_step):
    db, t_new, _ = q.shape
    page = cache_kt.shape[-1]
    n_pages = page_table.shape[1]
    pages_per_block = MOBA_BLOCK // page
    assert page == 128 and MOBA_BLOCK % page == 0
    assert (n_pages * page) % MOBA_BLOCK == 0 and t_new <= 128
    n_blk = n_pages // pages_per_block
    assert MOBA_TOPK <= n_blk <= 128 and n_blk % blocks_per_step == 0
    assert MOBA_BLOCK + 1 >= FAR_DISTANCE
    rows = N_HEADS * t_new
    q_rows = jnp.tile(q, (1, N_HEADS, 1))
    pages_per_step = blocks_per_step * pages_per_block

    def page_spec(j):
        return pl.BlockSpec((None, None, N_HEADS, HEAD_DIM, page),
                            lambda b, s, pt, j=j: (layer, pt[b, s * pages_per_step + j], 0, 0, 0))

    tok_spec = pl.BlockSpec((None, t_new, D_ATTN), lambda b, s, pt: (b, 0, 0))
    grid_spec = pltpu.PrefetchScalarGridSpec(
        num_scalar_prefetch=1,
        grid=(db, n_blk // blocks_per_step),
        in_specs=[pl.BlockSpec((None, rows, D_ATTN), lambda b, s, pt: (b, 0, 0)), tok_spec, tok_spec,
                  pl.BlockSpec(sbias.shape, lambda b, s, pt: (0, 0))]
                 + [page_spec(j) for j in range(pages_per_step)] * 2,
        out_specs=tok_spec,
        scratch_shapes=[pltpu.VMEM((n_blk, rows, page), F32), pltpu.VMEM((rows, 128), F32),
                        pltpu.VMEM((n_blk, rows, page), F32), pltpu.VMEM((n_blk, rows, D_ATTN), F32)],
    )
    return pl.pallas_call(
        functools.partial(_moba_sample_kernel, blocks_per_step=blocks_per_step,
                          pages_per_block=pages_per_block, n_blk=n_blk),
        grid_spec=grid_spec,
        out_shape=jax.ShapeDtypeStruct((db, t_new, D_ATTN), F32),
        compiler_params=pltpu.CompilerParams(dimension_semantics=("arbitrary", "arbitrary"),
                                             vmem_limit_bytes=VMEM_LIMIT_BYTES),
        name="moba_sample",
    )(page_table, q_rows, k_new, v_new, sbias, *([cache_kt] * pages_per_step), *([cache_vt] * pages_per_step))


def _merge_kernel(x_ref, attn_ref, ga_ref, u_ref, vn_ref, gb_ref, ma_ref, mb_ref,
                  mix_ref, sb_ref, wpa_ref, wpb_ref, wo_ref, fg_ref, o_ref, *, final):
    tm = x_ref.shape[0]
    d_sgu = u_ref.shape[-1]
    group_dim = d_sgu // N_GROUPS
    first_of_pair = lax.broadcasted_iota(jnp.int32, (CHUNK, 2 * group_dim), 1) < group_dim
    z_chunks = []
    for c in range(tm // CHUNK):
        vn = vn_ref[c * CHUNK:(c + 1) * CHUNK, :].astype(BF16)
        z_pairs = []
        for j in range(N_GROUPS // 2):
            v_pair = vn[:, j * 2 * group_dim:(j + 1) * 2 * group_dim]
            z_a = jnp.dot(mix_ref[2 * j], v_pair, preferred_element_type=F32)
            z_b = jnp.dot(mix_ref[2 * j + 1], v_pair, preferred_element_type=F32)
            z_pairs.append(jnp.where(first_of_pair, z_a, z_b))
        z_chunks.append(jnp.concatenate(z_pairs, axis=1) + sb_ref[...])
    z = jnp.concatenate(z_chunks, axis=0) if len(z_chunks) > 1 else z_chunks[0]
    y_b = (u_ref[...].astype(F32) * z) * _silu(gb_ref[...].astype(F32))
    y_a = attn_ref[...].astype(F32) * _silu(ga_ref[...].astype(F32))
    p_a = jnp.dot(y_a.astype(BF16), wpa_ref[...], preferred_element_type=F32)
    p_b = jnp.dot(y_b.astype(BF16), wpb_ref[...], preferred_element_type=F32)
    mixed = (jax.nn.sigmoid(ma_ref[...].astype(F32)) * p_a
             + jax.nn.sigmoid(mb_ref[...].astype(F32)) * p_b)
    out = x_ref[...] + jnp.dot(mixed.astype(BF16), wo_ref[...], preferred_element_type=F32)
    if final:
        out = out * lax.rsqrt(jnp.mean(out * out, axis=-1, keepdims=True) + EPS) * fg_ref[...]
    o_ref[...] = out


def _merge(x, attn, ga, u, vn, gb, ma, mb, mix_bf, sgu_bias, wpa_bf, wpb_bf, wo_bf, final_g, tm, final):
    n_tok, d_model = x.shape
    d_sgu = u.shape[-1]
    assert n_tok % tm == 0 and tm % CHUNK == 0
    row = lambda i: (i, 0)
    fixed2 = lambda i: (0, 0)
    narrow = pl.BlockSpec((tm, d_sgu), row)
    wide = pl.BlockSpec((tm, d_model), row)
    return pl.pallas_call(
        functools.partial(_merge_kernel, final=final),
        grid=(n_tok // tm,),
        in_specs=[wide, pl.BlockSpec((tm, D_ATTN), row), pl.BlockSpec((tm, D_ATTN), row),
                  narrow, narrow, narrow, wide, wide,
                  pl.BlockSpec(mix_bf.shape, lambda i: (0, 0, 0)),
                  pl.BlockSpec(sgu_bias.shape, fixed2),
                  pl.BlockSpec(wpa_bf.shape, fixed2),
                  pl.BlockSpec(wpb_bf.shape, fixed2),
                  pl.BlockSpec(wo_bf.shape, fixed2),
                  pl.BlockSpec((1, d_model), fixed2)],
        out_specs=wide,
        out_shape=jax.ShapeDtypeStruct((n_tok, d_model), F32),
        compiler_params=pltpu.CompilerParams(dimension_semantics=("arbitrary",),
                                             vmem_limit_bytes=VMEM_LIMIT_BYTES),
        name="merge",
    )(x, attn, ga, u, vn, gb, ma, mb, mix_bf, sgu_bias, wpa_bf, wpb_bf, wo_bf, final_g.reshape(1, d_model))


PROMPT_TOKEN_TILE = 256
SAMPLE_BLOCKS_PER_STEP = 8


def kernel(x_prompt, x_sample, cache_k, cache_v, page_table, rel_bias, norm_g, w_in, sgu_ln_g, sgu_ln_b,
           w_s, b_s, w_pa, w_pb, w_o, final_norm_g):
    b, s, d_model = x_prompt.shape
    db, t_new, _ = x_sample.shape
    depth = w_in.shape[0]
    d_sgu = sgu_ln_g.shape[-1]
    group_dim = d_sgu // N_GROUPS
    assert s % CHUNK == 0 and (db * t_new) % CHUNK == 0 and CHUNK % t_new == 0
    assert cache_k.shape[3] == N_HEADS and cache_k.shape[4] == HEAD_DIM

    bias_tiles = _prompt_bias_tiles(rel_bias)
    sbias = _sample_bias(rel_bias, t_new)
    cache_kt = jnp.transpose(cache_k, (0, 1, 3, 4, 2))
    cache_vt = jnp.transpose(cache_v, (0, 1, 3, 4, 2))

    xp = x_prompt.reshape(b * s, d_model)
    xs = x_sample.reshape(db * t_new, d_model)
    k_all = jnp.zeros((depth, b, D_ATTN, s), F32)
    v_all = jnp.zeros((depth, b, D_ATTN, s), F32)
    k_s, v_s, sgu_rows = [], [], []
    for l in range(depth):
        final = l == depth - 1
        wqkv = w_in[l][:, :3 * D_ATTN]
        wqkv_bf, wqkv_t_bf = wqkv.astype(BF16), wqkv.T.astype(BF16)
        w_bf = w_in[l][:, 3 * D_ATTN:].astype(BF16)
        wpa_bf, wpb_bf, wo_bf = w_pa[l].astype(BF16), w_pb[l].astype(BF16), w_o[l].astype(BF16)
        mix_p = jnp.tril(w_s[l]).astype(BF16)
        bias_p = jnp.repeat(b_s[l].T, group_dim, axis=1)
        corner = jnp.tril(w_s[l][:, :t_new, :t_new])
        eye = jnp.eye(CHUNK // t_new, dtype=F32)
        mix_s = jnp.einsum("ab,gts->gatbs", eye, corner).reshape(N_GROUPS, CHUNK, CHUNK).astype(BF16)
        bias_s = jnp.tile(jnp.repeat(b_s[l][:, :t_new].T, group_dim, axis=1), (CHUNK // t_new, 1))

        q_t, k_all, v_all, ga, u, vn, gb, ma, mb = _inproj(
            xp, norm_g[l], wqkv_t_bf, w_bf, sgu_ln_g[l], sgu_ln_b[l], PROMPT_TOKEN_TILE,
            kv_stack=(k_all, v_all), layer=l, seq_len=s)
        attn = _moba_prompt(rel_bias, q_t, k_all, v_all, l, bias_tiles)
        xp = _merge(xp, attn.reshape(b * s, D_ATTN), ga, u, vn, gb, ma, mb, mix_p, bias_p,
                    wpa_bf, wpb_bf, wo_bf, final_norm_g, PROMPT_TOKEN_TILE, final)

        q, k, v, ga, u, vn, gb, ma, mb = _inproj(xs, norm_g[l], wqkv_bf, w_bf, sgu_ln_g[l], sgu_ln_b[l], CHUNK)
        attn = _moba_sample(q.reshape(db, t_new, D_ATTN), k.reshape(db, t_new, D_ATTN), v.reshape(db, t_new, D_ATTN),
                            cache_kt, cache_vt, l, page_table, sbias, SAMPLE_BLOCKS_PER_STEP)
        xs = _merge(xs, attn.reshape(db * t_new, D_ATTN), ga, u, vn, gb, ma, mb, mix_s, bias_s,
                    wpa_bf, wpb_bf, wo_bf, final_norm_g, CHUNK, final)
        k_s.append(k.reshape(db, t_new, N_HEADS, HEAD_DIM))
        v_s.append(v.reshape(db, t_new, N_HEADS, HEAD_DIM))
        sgu_rows.append(vn.reshape(db, t_new, d_sgu))

    k_prompt = jnp.transpose(k_all.reshape(depth, b, N_HEADS, HEAD_DIM, s), (0, 1, 4, 2, 3))
    v_prompt = jnp.transpose(v_all.reshape(depth, b, N_HEADS, HEAD_DIM, s), (0, 1, 4, 2, 3))
    return (xp.reshape(b, s, d_model), xs.reshape(db, t_new, d_model),
            k_prompt, v_prompt, jnp.stack(k_s), jnp.stack(v_s), jnp.stack(sgu_rows))
```

```python
import functools
import math

import numpy as np
import jax
import jax.numpy as jnp
from jax import lax
from jax.experimental import pallas as pl
from jax.experimental.pallas import tpu as pltpu

N_HEADS = 8
HEAD_DIM = 64
D_ATTN = N_HEADS * HEAD_DIM
MOBA_BLOCK = 256
MOBA_TOPK = 3
N_GROUPS = 8
CHUNK = 128
N_BUCKETS = 32
MAX_DISTANCE = 128
EPS = 1e-6
LN_EPS = 1e-5
SCALE = HEAD_DIM ** -0.5
LOG2_E = math.log2(math.e)

F32 = jnp.float32
BF16 = jnp.bfloat16
NEG_INF = float("-inf")
NT_DIMS = (((1,), (1,)), ((), ()))

VMEM_LIMIT_BYTES = 56 * 1024 * 1024


def _bucket_thresholds():
    max_exact = N_BUCKETS // 2
    d = np.arange(max_exact, 4 * MAX_DISTANCE, dtype=np.float64)
    log_b = max_exact + (np.log(d / max_exact) / math.log(MAX_DISTANCE / max_exact) * (N_BUCKETS - max_exact)).astype(np.int64)
    bucket = np.minimum(log_b, N_BUCKETS - 1)
    return [int(d[np.argmax(bucket >= b)]) for b in range(max_exact + 1, N_BUCKETS)]


BUCKET_THRESHOLDS = _bucket_thresholds()
FAR_DISTANCE = BUCKET_THRESHOLDS[-1]


def _rel_bucket(dist):
    max_exact = N_BUCKETS // 2
    log_b = jnp.full(dist.shape, max_exact, jnp.int32)
    for thr in BUCKET_THRESHOLDS:
        log_b = log_b + jnp.where(dist >= thr, 1, 0)
    return jnp.where(dist < max_exact, dist, log_b)


def _gelu_tanh(x):
    return 0.5 * x * (1.0 + jnp.tanh(math.sqrt(2.0 / math.pi) * (x + 0.044715 * (x * x * x))))


def _silu(x):
    return x * jax.nn.sigmoid(x)


def _prompt_bias_kernel(tab_ref, o_ref):
    h = pl.program_id(0)
    c = pl.program_id(1)
    shape = o_ref.shape
    key = lax.broadcasted_iota(jnp.int32, shape, 0)
    qry = lax.broadcasted_iota(jnp.int32, shape, 1)
    dist = c * MOBA_BLOCK + qry - key
    bucket = _rel_bucket(dist)
    val = jnp.full(shape, tab_ref[N_BUCKETS - 1, h], F32)
    for b in range(N_BUCKETS - 1):
        val = jnp.where(bucket == b, tab_ref[b, h], val)
    o_ref[...] = jnp.where(dist < 0, NEG_INF, val)


def _prompt_bias_tiles(rel_bias):
    assert MOBA_BLOCK + 1 >= FAR_DISTANCE
    return pl.pallas_call(
        _prompt_bias_kernel,
        grid=(N_HEADS, 2),
        in_specs=[pl.BlockSpec(memory_space=pltpu.SMEM)],
        out_specs=pl.BlockSpec((None, None, MOBA_BLOCK, MOBA_BLOCK), lambda h, c: (h, c, 0, 0)),
        out_shape=jax.ShapeDtypeStruct((N_HEADS, 2, MOBA_BLOCK, MOBA_BLOCK), F32),
        compiler_params=pltpu.CompilerParams(dimension_semantics=("arbitrary", "arbitrary")),
        name="prompt_bias_tiles",
    )(rel_bias)


def _sample_bias_kernel(tab_ref, o_ref, *, t_new):
    shape = o_ref.shape
    row = lax.broadcasted_iota(jnp.int32, shape, 0)
    lane = lax.broadcasted_iota(jnp.int32, shape, 1)
    t = row % t_new
    dist = jnp.where(lane < MOBA_BLOCK, MOBA_BLOCK + t - lane,
                     jnp.where(lane < MOBA_BLOCK + 128, t - (lane - MOBA_BLOCK), FAR_DISTANCE))
    bucket = _rel_bucket(dist)
    tab = tab_ref[...]
    val = jnp.broadcast_to(tab[:, N_BUCKETS - 1:N_BUCKETS], shape)
    for b in range(N_BUCKETS - 1):
        val = jnp.where(bucket == b, tab[:, b:b + 1], val)
    o_ref[...] = jnp.where(dist < 0, NEG_INF, val)


def _sample_bias(rel_bias, t_new):
    rows = N_HEADS * t_new
    tab_rows = jnp.repeat(rel_bias.T, t_new, axis=0)
    return pl.pallas_call(
        functools.partial(_sample_bias_kernel, t_new=t_new),
        out_shape=jax.ShapeDtypeStruct((rows, MOBA_BLOCK + 256), F32),
        name="sample_bias",
    )(tab_rows)


def _inproj_kernel(*refs, transposed_qkv, n_aliased):
    x_ref, g_ref, wqkv_ref, w_ref, lng_ref, lnb_ref = refs[:6]
    q_ref, k_ref, v_ref, ga_ref, u_ref, vn_ref, gb_ref, ma_ref, mb_ref = refs[6 + n_aliased:]
    x = x_ref[...]
    d_model = x.shape[-1]
    d_sgu = u_ref.shape[-1]
    h = (x * lax.rsqrt(jnp.mean(x * x, axis=-1, keepdims=True) + EPS) * g_ref[...]).astype(BF16)

    if transposed_qkv:
        qkv_t = lax.dot_general(wqkv_ref[...], h, NT_DIMS, preferred_element_type=F32)
        q_ref[...] = qkv_t[0:D_ATTN].astype(q_ref.dtype)
        for ref, rows in ((k_ref, qkv_t[D_ATTN:2 * D_ATTN]), (v_ref, qkv_t[2 * D_ATTN:3 * D_ATTN])):
            if len(ref.shape) == 2:
                ref[...] = rows
            else:
                ref[0] = rows
                for later in range(1, ref.shape[0]):
                    ref[later] = jnp.zeros_like(rows)
    else:
        qkv = jnp.dot(h, wqkv_ref[...], preferred_element_type=F32)
        q_ref[...] = qkv[:, 0:D_ATTN]
        k_ref[...] = qkv[:, D_ATTN:2 * D_ATTN]
        v_ref[...] = qkv[:, 2 * D_ATTN:3 * D_ATTN]

    def seg(lo, width):
        return jnp.dot(h, w_ref[:, lo:lo + width], preferred_element_type=F32)

    act = ga_ref.dtype
    ga_ref[...] = seg(0, D_ATTN).astype(act)
    base = D_ATTN
    u_ref[...] = _gelu_tanh(seg(base, d_sgu)).astype(act)
    vs = _gelu_tanh(seg(base + d_sgu, d_sgu))
    mu = jnp.mean(vs, axis=-1, keepdims=True)
    var = jnp.mean(jnp.square(vs - mu), axis=-1, keepdims=True)
    vn_ref[...] = ((vs - mu) * lax.rsqrt(var + LN_EPS) * lng_ref[...] + lnb_ref[...]).astype(act)
    gb_ref[...] = seg(base + 2 * d_sgu, d_sgu).astype(act)
    ma_ref[...] = seg(base + 3 * d_sgu, d_model).astype(act)
    mb_ref[...] = seg(base + 3 * d_sgu + d_model, d_model).astype(act)


def _inproj(x, norm_g, wqkv_bf, w_bf, ln_g, ln_b, tm, stack_shape=None, kv_stack=None, layer=None, seq_len=None):
    n_tok, d_model = x.shape
    d_sgu = ln_g.shape[-1]
    assert n_tok % tm == 0
    widths = [D_ATTN] + [d_sgu] * 3 + [d_model] * 2
    assert sum(widths) == w_bf.shape[1]
    row = lambda i: (i, 0)
    fixed = lambda i: (0, 0)
    transposed = stack_shape is not None
    in_specs = [pl.BlockSpec((tm, d_model), row),
                pl.BlockSpec((1, d_model), fixed),
                pl.BlockSpec(wqkv_bf.shape, fixed),
                pl.BlockSpec(w_bf.shape, fixed),
                pl.BlockSpec((1, d_sgu), fixed),
                pl.BlockSpec((1, d_sgu), fixed)]
    args = [x, norm_g.reshape(1, d_model), wqkv_bf, w_bf, ln_g.reshape(1, d_sgu), ln_b.reshape(1, d_sgu)]
    rest_specs = [pl.BlockSpec((tm, w), row) for w in widths]
    act_dtype = BF16 if transposed else F32
    rest_shapes = [jax.ShapeDtypeStruct((n_tok, w), act_dtype) for w in widths]
    aliases = {}
    if transposed:
        assert seq_len % tm == 0 and n_tok % seq_len == 0
        tiles = seq_len // tm
        batch = n_tok // seq_len
        if kv_stack is not None:
            in_specs += [pl.BlockSpec(memory_space=pl.ANY)] * 2
            args += list(kv_stack)
            aliases = {6: 1, 7: 2}
            stack_spec = pl.BlockSpec((None, None, D_ATTN, tm), lambda i: (layer, i // tiles, 0, i % tiles))
        else:
            assert layer == 0
            stack_spec = pl.BlockSpec((stack_shape[0], None, D_ATTN, tm), lambda i: (0, i // tiles, 0, i % tiles))
        qkv_specs = [pl.BlockSpec((None, D_ATTN, tm), lambda i: (i // tiles, 0, i % tiles)), stack_spec, stack_spec]
        qkv_shapes = [jax.ShapeDtypeStruct((batch, D_ATTN, seq_len), BF16),
                      jax.ShapeDtypeStruct(stack_shape, F32), jax.ShapeDtypeStruct(stack_shape, F32)]
    else:
        qkv_specs = [pl.BlockSpec((tm, D_ATTN), row)] * 3
        qkv_shapes = [jax.ShapeDtypeStruct((n_tok, D_ATTN), F32)] * 3
    return pl.pallas_call(
        functools.partial(_inproj_kernel, transposed_qkv=transposed, n_aliased=len(aliases)),
        grid=(n_tok // tm,),
        in_specs=in_specs,
        out_specs=qkv_specs + rest_specs,
        out_shape=qkv_shapes + rest_shapes,
        input_output_aliases=aliases,
        compiler_params=pltpu.CompilerParams(dimension_semantics=("arbitrary",),
                                             vmem_limit_bytes=VMEM_LIMIT_BYTES),
        name="inproj",
    )(*args)


def _moba_prompt_kernel(tab_ref, qt_ref, kt_ref, vt_ref, bias_ref, o_ref,
                        kb_scr, vb_scr, kmbd_scr, neg_scr, qh_scr, s_scr, m_scr, l_scr, acc_scr):
    qt = pl.program_id(1)
    n_blk = kb_scr.shape[0]
    tq = qt_ref.shape[1]

    @pl.when(qt == 0)
    def _():
        lane_head = lax.broadcasted_iota(jnp.int32, (1, D_ATTN), 1) // HEAD_DIM
        for n in range(n_blk):
            kblk = kt_ref[:, n * MOBA_BLOCK:(n + 1) * MOBA_BLOCK].T
            kb_scr[n] = kblk.astype(BF16)
            vb_scr[n] = vt_ref[:, n * MOBA_BLOCK:(n + 1) * MOBA_BLOCK].astype(BF16)
            k_mean = jnp.sum(kblk, axis=0, keepdims=True) * (1.0 / MOBA_BLOCK)
            for h in range(N_HEADS):
                kmbd_scr[h * n_blk + n:h * n_blk + n + 1, :] = jnp.where(lane_head == h, k_mean, 0.0)

    q_t = (qt_ref[...].astype(F32) * SCALE).astype(BF16)

    gate = jnp.dot(kmbd_scr[...].astype(BF16), q_t, preferred_element_type=F32)
    blk = lax.broadcasted_iota(jnp.int32, (n_blk, tq), 0)
    half = lax.broadcasted_iota(jnp.int32, (2 * HEAD_DIM, tq), 0) // HEAD_DIM
    for h in range(N_HEADS):
        g = gate[h * n_blk:(h + 1) * n_blk, :]
        beaten = jnp.zeros((n_blk, tq), jnp.int32)
        for m in range(n_blk):
            gm = g[m:m + 1, :]
            wins = (gm > g) | ((gm == g) & (m < blk))
            beaten = beaten + jnp.where(wins, 1, 0) * (m < qt).astype(jnp.int32)
        selected = (blk < qt) & (beaten < MOBA_TOPK)
        far = jnp.where(blk < qt - 1, tab_ref[N_BUCKETS - 1, h], 0.0)
        neg_scr[h] = jnp.where(selected, far, NEG_INF)
        pair = h // 2
        q_pair = q_t[pair * 2 * HEAD_DIM:(pair + 1) * 2 * HEAD_DIM, :]
        qh_scr[h] = jnp.where(half == h % 2, q_pair, jnp.zeros_like(q_pair))

    def scores(h, n):
        pair = h // 2
        k_pair = kb_scr[n, :, pair * 2 * HEAD_DIM:(pair + 1) * 2 * HEAD_DIM]
        return jnp.dot(k_pair, qh_scr[h], preferred_element_type=F32)

    def sublane_groups(a):
        return a.reshape(a.shape[0] // 8, 8, a.shape[1])

    n_group = s_scr.shape[0]
    for g in range(N_HEADS // n_group):
        heads = range(g * n_group, (g + 1) * n_group)

        def extra_bias(h, n, kind):
            if kind == "own":
                return bias_ref[h, 0]
            row = neg_scr[h, pl.ds(n, 1), :]
            return bias_ref[h, 1] + row if kind == "prev" else row

        def score_blocks(blocks):
            for h in heads:
                part = m_scr[h]
                for n, kind in blocks:
                    s = (scores(h, n) + extra_bias(h, n, kind)) * LOG2_E
                    s_scr[h % n_group, n] = s
                    part = jnp.maximum(part, jnp.max(sublane_groups(s), axis=0))
                m_scr[h] = part

        for h in heads:
            m_scr[h] = jnp.full(m_scr.shape[1:], NEG_INF, F32)
        n_far = qt - 1

        @pl.when(qt == 0)
        def _():
            score_blocks([(qt, "own")])

        @pl.when((qt >= 1) & (n_far % 2 == 0))
        def _():
            score_blocks([(qt, "own"), (qt - 1, "prev")])

        @pl.when((qt >= 1) & (n_far % 2 == 1))
        def _():
            score_blocks([(qt, "own"), (qt - 1, "prev"), (qt - 2, "far")])

        def far_body(i, carry):
            score_blocks([(2 * i, "far"), (2 * i + 1, "far")])
            return carry

        lax.fori_loop(0, jnp.maximum(n_far, 0) // 2, far_body, 0)

        m_fin = {h: jnp.max(m_scr[h], axis=0, keepdims=True) for h in heads}
        for h in heads:
            l_scr[h] = jnp.zeros(l_scr.shape[1:], F32)
            acc_scr[h * HEAD_DIM:(h + 1) * HEAD_DIM, :] = jnp.zeros((HEAD_DIM, tq), F32)

        def weight_blocks(blocks):
            for h in heads:
                rows = slice(h * HEAD_DIM, (h + 1) * HEAD_DIM)
                l_part = l_scr[h]
                acc = acc_scr[rows, :]
                for n in blocks:
                    p = jnp.exp2(s_scr[h % n_group, n] - m_fin[h])
                    l_part = l_part + jnp.sum(sublane_groups(p), axis=0)
                    v_t = vb_scr[n, rows, :]
                    acc = acc + jnp.dot(v_t, p.astype(BF16), preferred_element_type=F32)
                l_scr[h] = l_part
                acc_scr[rows, :] = acc

        n_all = qt + 1

        @pl.when(n_all % 2 == 1)
        def _():
            weight_blocks([qt])

        def weight_body(i, carry):
            weight_blocks([2 * i, 2 * i + 1])
            return carry

        lax.fori_loop(0, n_all // 2, weight_body, 0)

        for h in heads:
            rows = slice(h * HEAD_DIM, (h + 1) * HEAD_DIM)
            acc_scr[rows, :] = acc_scr[rows, :] / jnp.sum(l_scr[h], axis=0, keepdims=True)
    o_ref[...] = acc_scr[...].T.astype(o_ref.dtype)


PROMPT_HEAD_GROUP = 8


def _moba_prompt(rel_bias, q_t, k_all, v_all, layer, bias_tiles):
    b, _, s = q_t.shape
    assert s % MOBA_BLOCK == 0
    n_blk = s // MOBA_BLOCK
    tq = MOBA_BLOCK
    stack_spec = pl.BlockSpec((None, None, D_ATTN, s), lambda i, j: (layer, i, 0, 0))
    return pl.pallas_call(
        _moba_prompt_kernel,
        grid=(b, s // tq),
        in_specs=[pl.BlockSpec(memory_space=pltpu.SMEM),
                  pl.BlockSpec((None, D_ATTN, tq), lambda i, j: (i, 0, j)),
                  stack_spec, stack_spec,
                  pl.BlockSpec(bias_tiles.shape, lambda i, j: (0, 0, 0, 0))],
        out_specs=pl.BlockSpec((None, tq, D_ATTN), lambda i, j: (i, j, 0)),
        out_shape=jax.ShapeDtypeStruct((b, s, D_ATTN), BF16),
        scratch_shapes=[pltpu.VMEM((n_blk, MOBA_BLOCK, D_ATTN), BF16),
                        pltpu.VMEM((n_blk, D_ATTN, MOBA_BLOCK), BF16),
                        pltpu.VMEM((N_HEADS * n_blk, D_ATTN), F32),
                        pltpu.VMEM((N_HEADS, n_blk, tq), F32),
                        pltpu.VMEM((N_HEADS, 2 * HEAD_DIM, tq), BF16),
                        pltpu.VMEM((PROMPT_HEAD_GROUP, n_blk, MOBA_BLOCK, tq), F32),
                        pltpu.VMEM((N_HEADS, 8, tq), F32),
                        pltpu.VMEM((N_HEADS, 8, tq), F32),
                        pltpu.VMEM((D_ATTN, tq), F32)],
        compiler_params=pltpu.CompilerParams(dimension_semantics=("arbitrary", "arbitrary"),
                                             vmem_limit_bytes=VMEM_LIMIT_BYTES),
        name="moba_prompt",
    )(rel_bias, q_t, k_all, v_all, bias_tiles)


def _moba_sample_kernel(pt_ref, q_ref, kn_ref, vn_ref, sb_ref, *refs, blocks_per_step, pages_per_block, n_blk):
    n_pages = blocks_per_step * pages_per_block
    k_pages = refs[:n_pages]
    v_pages = refs[n_pages:2 * n_pages]
    o_ref = refs[2 * n_pages]
    g_scr, m_scr, l_scr, o_scr = refs[2 * n_pages + 1:]
    del pt_ref
    step = pl.program_id(1)
    t_new = kn_ref.shape[0]
    rows = q_ref.shape[0]
    page = k_pages[0].shape[-1]

    row_head = lax.broadcasted_iota(jnp.int32, (rows, D_ATTN), 0) // t_new
    lane_head = lax.broadcasted_iota(jnp.int32, (rows, D_ATTN), 1) // HEAD_DIM
    own_lanes = row_head == lane_head
    q_rows = jnp.where(own_lanes, q_ref[...] * SCALE, 0.0)
    q_bf = q_rows.astype(BF16)
    lane = lax.broadcasted_iota(jnp.int32, (rows, 128), 1)

    @pl.when(step == 0)
    def _():
        m_scr[...] = jnp.full(m_scr.shape, NEG_INF, F32)

    def block_t(pages, jb):
        parts = [pages[jb * pages_per_block + jp][...].reshape(D_ATTN, page) for jp in range(pages_per_block)]
        return jnp.concatenate(parts, axis=1).astype(BF16)

    def fold_pages(a):
        return sum(a[:, jp * page:(jp + 1) * page] for jp in range(pages_per_block))

    far_bias = sb_ref[:, MOBA_BLOCK + 128:MOBA_BLOCK + 129]
    m_all = m_scr[...]
    blocks = [step * blocks_per_step + jb for jb in range(blocks_per_step)]
    scores = [jnp.dot(q_bf, block_t(k_pages, jb), preferred_element_type=F32)
              for jb in range(blocks_per_step)]
    probs = []
    for n, s in zip(blocks, scores):
        g_scr[n] = fold_pages(s)
        newest = (jnp.zeros(s.shape, jnp.int32) + n) == n_blk - 1
        s = s + jnp.where(newest, sb_ref[:, 0:MOBA_BLOCK], far_bias)
        m_blk = jnp.max(s, axis=1, keepdims=True)
        p = jnp.exp(s - m_blk)
        l_scr[n] = fold_pages(p)
        m_all = jnp.where(lane == n, m_blk, m_all)
        probs.append(p.astype(BF16))
    m_scr[...] = m_all
    for jb, (n, p) in enumerate(zip(blocks, probs)):
        o_scr[n] = lax.dot_general(p, block_t(v_pages, jb), NT_DIMS, preferred_element_type=F32)

    @pl.when(step == pl.num_programs(1) - 1)
    def _():
        g = jnp.full((rows, 128), NEG_INF, F32)
        l_all = jnp.zeros((rows, 128), F32)
        for n in range(n_blk):
            g = jnp.where(lane == n, jnp.sum(g_scr[n], axis=1, keepdims=True), g)
            l_all = jnp.where(lane == n, jnp.sum(l_scr[n], axis=1, keepdims=True), l_all)
        beaten = jnp.zeros(g.shape, jnp.int32)
        for m in range(n_blk):
            gm = g[:, m:m + 1]
            beaten = beaten + jnp.where((gm > g) | ((gm == g) & (m < lane)), 1, 0)
        selected = (beaten < MOBA_TOPK) & (lane < n_blk)
        m_sel = jnp.where(selected, m_scr[...], NEG_INF)

        s_loc = jnp.full((rows, 128), NEG_INF, F32)
        for t in range(t_new):
            col = jnp.sum(q_rows * kn_ref[t:t + 1, :], axis=1, keepdims=True)
            s_loc = jnp.where(lane == t, col, s_loc)
        s_loc = s_loc + sb_ref[:, MOBA_BLOCK:MOBA_BLOCK + 128]

        m_tot = jnp.maximum(jnp.max(m_sel, axis=1, keepdims=True), jnp.max(s_loc, axis=1, keepdims=True))
        w_blk = jnp.where(selected, jnp.exp(m_sel - m_tot), 0.0)
        p_loc = jnp.exp(s_loc - m_tot)
        denom = jnp.sum(w_blk * l_all, axis=1, keepdims=True) + jnp.sum(p_loc, axis=1, keepdims=True)
        acc = jnp.zeros((rows, D_ATTN), F32)
        for n in range(n_blk):
            acc = acc + w_blk[:, n:n + 1] * o_scr[n]
        for t in range(t_new):
            acc = acc + p_loc[:, t:t + 1] * vn_ref[t:t + 1, :]
        acc = jnp.where(own_lanes, acc / denom, 0.0)
        folded = acc
        for h in range(1, N_HEADS):
            folded = folded + pltpu.roll(acc, rows - h * t_new, 0)
        o_ref[...] = folded[:t_new, :]


def _moba_sample(q, k_new, v_new, cache_kt, cache_vt, layer, page_table, sbias, blocks_per_step):
    db, t_new, _ = q.shape
    page = cache_kt.shape[-1]
    n_pages = page_table.shape[1]
    pages_per_block = MOBA_BLOCK // page
    assert page == 128 and MOBA_BLOCK % page == 0
    assert (n_pages * page) % MOBA_BLOCK == 0 and t_new <= 128
    n_blk = n_pages // pages_per_block
    assert MOBA_TOPK <= n_blk <= 128 and n_blk % blocks_per_step == 0
    assert MOBA_BLOCK + 1 >= FAR_DISTANCE
    rows = N_HEADS * t_new
    q_rows = jnp.tile(q, (1, N_HEADS, 1))
    pages_per_step = blocks_per_step * pages_per_block

    def page_spec(j):
        return pl.BlockSpec((None, None, N_HEADS, HEAD_DIM, page),
                            lambda b, s, pt, j=j: (layer, pt[b, s * pages_per_step + j], 0, 0, 0))

    tok_spec = pl.BlockSpec((None, t_new, D_ATTN), lambda b, s, pt: (b, 0, 0))
    grid_spec = pltpu.PrefetchScalarGridSpec(
        num_scalar_prefetch=1,
        grid=(db, n_blk // blocks_per_step),
        in_specs=[pl.BlockSpec((None, rows, D_ATTN), lambda b, s, pt: (b, 0, 0)), tok_spec, tok_spec,
                  pl.BlockSpec(sbias.shape, lambda b, s, pt: (0, 0))]
                 + [page_spec(j) for j in range(pages_per_step)] * 2,
        out_specs=tok_spec,
        scratch_shapes=[pltpu.VMEM((n_blk, rows, page), F32), pltpu.VMEM((rows, 128), F32),
                        pltpu.VMEM((n_blk, rows, page), F32), pltpu.VMEM((n_blk, rows, D_ATTN), F32)],
    )
    return pl.pallas_call(
        functools.partial(_moba_sample_kernel, blocks_per_step=blocks_per_step,
                          pages_per_block=pages_per_block, n_blk=n_blk),
        grid_spec=grid_spec,
        out_shape=jax.ShapeDtypeStruct((db, t_new, D_ATTN), F32),
        compiler_params=pltpu.CompilerParams(dimension_semantics=("arbitrary", "arbitrary"),
                                             vmem_limit_bytes=VMEM_LIMIT_BYTES),
        name="moba_sample",
    )(page_table, q_rows, k_new, v_new, sbias, *([cache_kt] * pages_per_step), *([cache_vt] * pages_per_step))


def _merge_kernel(x_ref, attn_ref, ga_ref, u_ref, vn_ref, gb_ref, ma_ref, mb_ref,
                  mix_ref, sb_ref, wpa_ref, wpb_ref, wo_ref, fg_ref, o_ref, *, final):
    tm = x_ref.shape[0]
    d_sgu = u_ref.shape[-1]
    group_dim = d_sgu // N_GROUPS
    first_of_pair = lax.broadcasted_iota(jnp.int32, (CHUNK, 2 * group_dim), 1) < group_dim
    z_chunks = []
    for c in range(tm // CHUNK):
        vn = vn_ref[c * CHUNK:(c + 1) * CHUNK, :].astype(BF16)
        z_pairs = []
        for j in range(N_GROUPS // 2):
            v_pair = vn[:, j * 2 * group_dim:(j + 1) * 2 * group_dim]
            z_a = jnp.dot(mix_ref[2 * j], v_pair, preferred_element_type=F32)
            z_b = jnp.dot(mix_ref[2 * j + 1], v_pair, preferred_element_type=F32)
            z_pairs.append(jnp.where(first_of_pair, z_a, z_b))
        z_chunks.append(jnp.concatenate(z_pairs, axis=1) + sb_ref[...])
    z = jnp.concatenate(z_chunks, axis=0) if len(z_chunks) > 1 else z_chunks[0]
    y_b = (u_ref[...].astype(F32) * z) * _silu(gb_ref[...].astype(F32))
    y_a = attn_ref[...].astype(F32) * _silu(ga_ref[...].astype(F32))
    p_a = jnp.dot(y_a.astype(BF16), wpa_ref[...], preferred_element_type=F32)
    p_b = jnp.dot(y_b.astype(BF16), wpb_ref[...], preferred_element_type=F32)
    mixed = (jax.nn.sigmoid(ma_ref[...].astype(F32)) * p_a
             + jax.nn.sigmoid(mb_ref[...].astype(F32)) * p_b)
    out = x_ref[...] + jnp.dot(mixed.astype(BF16), wo_ref[...], preferred_element_type=F32)
    if final:
        out = out * lax.rsqrt(jnp.mean(out * out, axis=-1, keepdims=True) + EPS) * fg_ref[...]
    o_ref[...] = out


def _merge(x, attn, ga, u, vn, gb, ma, mb, mix_bf, sgu_bias, wpa_bf, wpb_bf, wo_bf, final_g, tm, final):
    n_tok, d_model = x.shape
    d_sgu = u.shape[-1]
    assert n_tok % tm == 0 and tm % CHUNK == 0
    row = lambda i: (i, 0)
    fixed2 = lambda i: (0, 0)
    narrow = pl.BlockSpec((tm, d_sgu), row)
    wide = pl.BlockSpec((tm, d_model), row)
    return pl.pallas_call(
        functools.partial(_merge_kernel, final=final),
        grid=(n_tok // tm,),
        in_specs=[wide, pl.BlockSpec((tm, D_ATTN), row), pl.BlockSpec((tm, D_ATTN), row),
                  narrow, narrow, narrow, wide, wide,
                  pl.BlockSpec(mix_bf.shape, lambda i: (0, 0, 0)),
                  pl.BlockSpec(sgu_bias.shape, fixed2),
                  pl.BlockSpec(wpa_bf.shape, fixed2),
                  pl.BlockSpec(wpb_bf.shape, fixed2),
                  pl.BlockSpec(wo_bf.shape, fixed2),
                  pl.BlockSpec((1, d_model), fixed2)],
        out_specs=wide,
        out_shape=jax.ShapeDtypeStruct((n_tok, d_model), F32),
        compiler_params=pltpu.CompilerParams(dimension_semantics=("arbitrary",),
                                             vmem_limit_bytes=VMEM_LIMIT_BYTES),
        name="merge",
    )(x, attn, ga, u, vn, gb, ma, mb, mix_bf, sgu_bias, wpa_bf, wpb_bf, wo_bf, final_g.reshape(1, d_model))


PROMPT_TOKEN_TILE = 256
MERGE_TOKEN_TILE = 512
SAMPLE_BLOCKS_PER_STEP = 8


def kernel(x_prompt, x_sample, cache_k, cache_v, page_table, rel_bias, norm_g, w_in, sgu_ln_g, sgu_ln_b,
           w_s, b_s, w_pa, w_pb, w_o, final_norm_g):
    b, s, d_model = x_prompt.shape
    db, t_new, _ = x_sample.shape
    depth = w_in.shape[0]
    d_sgu = sgu_ln_g.shape[-1]
    group_dim = d_sgu // N_GROUPS
    assert s % CHUNK == 0 and (db * t_new) % CHUNK == 0 and CHUNK % t_new == 0
    assert cache_k.shape[3] == N_HEADS and cache_k.shape[4] == HEAD_DIM

    bias_tiles = _prompt_bias_tiles(rel_bias)
    sbias = _sample_bias(rel_bias, t_new)
    cache_kt = jnp.transpose(cache_k, (0, 1, 3, 4, 2))
    cache_vt = jnp.transpose(cache_v, (0, 1, 3, 4, 2))

    xp = x_prompt.reshape(b * s, d_model)
    xs = x_sample.reshape(db * t_new, d_model)
    kv_stack = None
    k_s, v_s, sgu_rows = [], [], []
    for l in range(depth):
        final = l == depth - 1
        wqkv = w_in[l][:, :3 * D_ATTN]
        wqkv_bf, wqkv_t_bf = wqkv.astype(BF16), wqkv.T.astype(BF16)
        w_bf = w_in[l][:, 3 * D_ATTN:].astype(BF16)
        wpa_bf, wpb_bf, wo_bf = w_pa[l].astype(BF16), w_pb[l].astype(BF16), w_o[l].astype(BF16)
        mix_p = jnp.tril(w_s[l]).astype(BF16)
        bias_p = jnp.repeat(b_s[l].T, group_dim, axis=1)
        corner = jnp.tril(w_s[l][:, :t_new, :t_new])
        eye = jnp.eye(CHUNK // t_new, dtype=F32)
        mix_s = jnp.einsum("ab,gts->gatbs", eye, corner).reshape(N_GROUPS, CHUNK, CHUNK).astype(BF16)
        bias_s = jnp.tile(jnp.repeat(b_s[l][:, :t_new].T, group_dim, axis=1), (CHUNK // t_new, 1))

        q_t, k_all, v_all, ga, u, vn, gb, ma, mb = _inproj(
            xp, norm_g[l], wqkv_t_bf, w_bf, sgu_ln_g[l], sgu_ln_b[l], PROMPT_TOKEN_TILE,
            stack_shape=(depth, b, D_ATTN, s), kv_stack=kv_stack, layer=l, seq_len=s)
        kv_stack = (k_all, v_all)
        attn = _moba_prompt(rel_bias, q_t, k_all, v_all, l, bias_tiles)
        xp = _merge(xp, attn.reshape(b * s, D_ATTN), ga, u, vn, gb, ma, mb, mix_p, bias_p,
                    wpa_bf, wpb_bf, wo_bf, final_norm_g, MERGE_TOKEN_TILE, final)

        q, k, v, ga, u, vn, gb, ma, mb = _inproj(xs, norm_g[l], wqkv_bf, w_bf, sgu_ln_g[l], sgu_ln_b[l], CHUNK)
        attn = _moba_sample(q.reshape(db, t_new, D_ATTN), k.reshape(db, t_new, D_ATTN), v.reshape(db, t_new, D_ATTN),
                            cache_kt, cache_vt, l, page_table, sbias, SAMPLE_BLOCKS_PER_STEP)
        xs = _merge(xs, attn.reshape(db * t_new, D_ATTN), ga, u, vn, gb, ma, mb, mix_s, bias_s,
                    wpa_bf, wpb_bf, wo_bf, final_norm_g, CHUNK, final)
        k_s.append(k.reshape(db, t_new, N_HEADS, HEAD_DIM))
        v_s.append(v.reshape(db, t_new, N_HEADS, HEAD_DIM))
        sgu_rows.append(vn.reshape(db, t_new, d_sgu))

    k_prompt = jnp.transpose(k_all.reshape(depth, b, N_HEADS, HEAD_DIM, s), (0, 1, 4, 2, 3))
    v_prompt = jnp.transpose(v_all.reshape(depth, b, N_HEADS, HEAD_DIM, s), (0, 1, 4, 2, 3))
    return (xp.reshape(b, s, d_model), xs.reshape(db, t_new, d_model),
            k_prompt, v_prompt, jnp.stack(k_s), jnp.stack(v_s), jnp.stack(sgu_rows))
```

```python
import functools
import math

import numpy as np
import jax
import jax.numpy as jnp
from jax import lax
from jax.experimental import pallas as pl
from jax.experimental.pallas import tpu as pltpu

N_HEADS = 8
HEAD_DIM = 64
D_ATTN = N_HEADS * HEAD_DIM
MOBA_BLOCK = 256
MOBA_TOPK = 3
N_GROUPS = 8
CHUNK = 128
N_BUCKETS = 32
MAX_DISTANCE = 128
EPS = 1e-6
LN_EPS = 1e-5
SCALE = HEAD_DIM ** -0.5
LOG2_E = math.log2(math.e)

F32 = jnp.float32
BF16 = jnp.bfloat16
NEG_INF = float("-inf")
NT_DIMS = (((1,), (1,)), ((), ()))

VMEM_LIMIT_BYTES = 56 * 1024 * 1024


def _bucket_thresholds():
    max_exact = N_BUCKETS // 2
    d = np.arange(max_exact, 4 * MAX_DISTANCE, dtype=np.float64)
    log_b = max_exact + (np.log(d / max_exact) / math.log(MAX_DISTANCE / max_exact) * (N_BUCKETS - max_exact)).astype(np.int64)
    bucket = np.minimum(log_b, N_BUCKETS - 1)
    return [int(d[np.argmax(bucket >= b)]) for b in range(max_exact + 1, N_BUCKETS)]


BUCKET_THRESHOLDS = _bucket_thresholds()
FAR_DISTANCE = BUCKET_THRESHOLDS[-1]


def _rel_bucket(dist):
    max_exact = N_BUCKETS // 2
    log_b = jnp.full(dist.shape, max_exact, jnp.int32)
    for thr in BUCKET_THRESHOLDS:
        log_b = log_b + jnp.where(dist >= thr, 1, 0)
    return jnp.where(dist < max_exact, dist, log_b)


def _gelu_tanh(x):
    return 0.5 * x * (1.0 + jnp.tanh(math.sqrt(2.0 / math.pi) * (x + 0.044715 * (x * x * x))))


def _silu(x):
    return x * jax.nn.sigmoid(x)


def _prompt_bias_kernel(tab_ref, o_ref):
    h = pl.program_id(0)
    c = pl.program_id(1)
    shape = o_ref.shape
    key = lax.broadcasted_iota(jnp.int32, shape, 0)
    qry = lax.broadcasted_iota(jnp.int32, shape, 1)
    dist = c * MOBA_BLOCK + qry - key
    bucket = _rel_bucket(dist)
    val = jnp.full(shape, tab_ref[N_BUCKETS - 1, h], F32)
    for b in range(N_BUCKETS - 1):
        val = jnp.where(bucket == b, tab_ref[b, h], val)
    o_ref[...] = jnp.where(dist < 0, NEG_INF, val)


def _prompt_bias_tiles(rel_bias):
    assert MOBA_BLOCK + 1 >= FAR_DISTANCE
    return pl.pallas_call(
        _prompt_bias_kernel,
        grid=(N_HEADS, 2),
        in_specs=[pl.BlockSpec(memory_space=pltpu.SMEM)],
        out_specs=pl.BlockSpec((None, None, MOBA_BLOCK, MOBA_BLOCK), lambda h, c: (h, c, 0, 0)),
        out_shape=jax.ShapeDtypeStruct((N_HEADS, 2, MOBA_BLOCK, MOBA_BLOCK), F32),
        compiler_params=pltpu.CompilerParams(dimension_semantics=("arbitrary", "arbitrary")),
        name="prompt_bias_tiles",
    )(rel_bias)


def _sample_bias_kernel(tab_ref, o_ref, *, t_new):
    shape = o_ref.shape
    row = lax.broadcasted_iota(jnp.int32, shape, 0)
    lane = lax.broadcasted_iota(jnp.int32, shape, 1)
    t = row % t_new
    dist = jnp.where(lane < MOBA_BLOCK, MOBA_BLOCK + t - lane,
                     jnp.where(lane < MOBA_BLOCK + 128, t - (lane - MOBA_BLOCK), FAR_DISTANCE))
    bucket = _rel_bucket(dist)
    tab = tab_ref[...]
    val = jnp.broadcast_to(tab[:, N_BUCKETS - 1:N_BUCKETS], shape)
    for b in range(N_BUCKETS - 1):
        val = jnp.where(bucket == b, tab[:, b:b + 1], val)
    o_ref[...] = jnp.where(dist < 0, NEG_INF, val)


def _sample_bias(rel_bias, t_new):
    rows = N_HEADS * t_new
    tab_rows = jnp.repeat(rel_bias.T, t_new, axis=0)
    return pl.pallas_call(
        functools.partial(_sample_bias_kernel, t_new=t_new),
        out_shape=jax.ShapeDtypeStruct((rows, MOBA_BLOCK + 256), F32),
        name="sample_bias",
    )(tab_rows)


def _inproj_kernel(*refs, transposed_qkv, n_aliased):
    x_ref, g_ref, wqkv_ref, w_ref, lng_ref, lnb_ref = refs[:6]
    q_ref, k_ref, v_ref, ga_ref, u_ref, vn_ref, gb_ref, ma_ref, mb_ref = refs[6 + n_aliased:]
    x = x_ref[...]
    d_model = x.shape[-1]
    d_sgu = u_ref.shape[-1]
    h = (x * lax.rsqrt(jnp.mean(x * x, axis=-1, keepdims=True) + EPS) * g_ref[...]).astype(BF16)

    if transposed_qkv:
        qkv_t = lax.dot_general(wqkv_ref[...], h, NT_DIMS, preferred_element_type=F32)
        q_ref[...] = qkv_t[0:D_ATTN].astype(q_ref.dtype)
        for ref, rows in ((k_ref, qkv_t[D_ATTN:2 * D_ATTN]), (v_ref, qkv_t[2 * D_ATTN:3 * D_ATTN])):
            if len(ref.shape) == 2:
                ref[...] = rows
            else:
                ref[0] = rows
                for later in range(1, ref.shape[0]):
                    ref[later] = jnp.zeros_like(rows)
    else:
        qkv = jnp.dot(h, wqkv_ref[...], preferred_element_type=F32)
        q_ref[...] = qkv[:, 0:D_ATTN]
        k_ref[...] = qkv[:, D_ATTN:2 * D_ATTN]
        v_ref[...] = qkv[:, 2 * D_ATTN:3 * D_ATTN]

    def seg(lo, width):
        return jnp.dot(h, w_ref[:, lo:lo + width], preferred_element_type=F32)

    act = ga_ref.dtype
    ga_ref[...] = seg(0, D_ATTN).astype(act)
    base = D_ATTN
    u_ref[...] = _gelu_tanh(seg(base, d_sgu)).astype(act)
    vs = _gelu_tanh(seg(base + d_sgu, d_sgu))
    mu = jnp.mean(vs, axis=-1, keepdims=True)
    var = jnp.mean(jnp.square(vs - mu), axis=-1, keepdims=True)
    vn_ref[...] = ((vs - mu) * lax.rsqrt(var + LN_EPS) * lng_ref[...] + lnb_ref[...]).astype(act)
    gb_ref[...] = seg(base + 2 * d_sgu, d_sgu).astype(act)
    ma_ref[...] = seg(base + 3 * d_sgu, d_model).astype(act)
    mb_ref[...] = seg(base + 3 * d_sgu + d_model, d_model).astype(act)


def _inproj(x, norm_g, wqkv_bf, w_bf, ln_g, ln_b, tm, stack_shape=None, kv_stack=None, layer=None, seq_len=None):
    n_tok, d_model = x.shape
    d_sgu = ln_g.shape[-1]
    assert n_tok % tm == 0
    widths = [D_ATTN] + [d_sgu] * 3 + [d_model] * 2
    assert sum(widths) == w_bf.shape[1]
    row = lambda i: (i, 0)
    fixed = lambda i: (0, 0)
    transposed = stack_shape is not None
    in_specs = [pl.BlockSpec((tm, d_model), row),
                pl.BlockSpec((1, d_model), fixed),
                pl.BlockSpec(wqkv_bf.shape, fixed),
                pl.BlockSpec(w_bf.shape, fixed),
                pl.BlockSpec((1, d_sgu), fixed),
                pl.BlockSpec((1, d_sgu), fixed)]
    args = [x, norm_g.reshape(1, d_model), wqkv_bf, w_bf, ln_g.reshape(1, d_sgu), ln_b.reshape(1, d_sgu)]
    rest_specs = [pl.BlockSpec((tm, w), row) for w in widths]
    act_dtype = BF16 if transposed else F32
    rest_shapes = [jax.ShapeDtypeStruct((n_tok, w), act_dtype) for w in widths]
    aliases = {}
    if transposed:
        assert seq_len % tm == 0 and n_tok % seq_len == 0
        tiles = seq_len // tm
        batch = n_tok // seq_len
        if kv_stack is not None:
            in_specs += [pl.BlockSpec(memory_space=pl.ANY)] * 2
            args += list(kv_stack)
            aliases = {6: 1, 7: 2}
            stack_spec = pl.BlockSpec((None, None, D_ATTN, tm), lambda i: (layer, i // tiles, 0, i % tiles))
        else:
            assert layer == 0
            stack_spec = pl.BlockSpec((stack_shape[0], None, D_ATTN, tm), lambda i: (0, i // tiles, 0, i % tiles))
        qkv_specs = [pl.BlockSpec((None, D_ATTN, tm), lambda i: (i // tiles, 0, i % tiles)), stack_spec, stack_spec]
        qkv_shapes = [jax.ShapeDtypeStruct((batch, D_ATTN, seq_len), BF16),
                      jax.ShapeDtypeStruct(stack_shape, F32), jax.ShapeDtypeStruct(stack_shape, F32)]
    else:
        qkv_specs = [pl.BlockSpec((tm, D_ATTN), row)] * 3
        qkv_shapes = [jax.ShapeDtypeStruct((n_tok, D_ATTN), F32)] * 3
    return pl.pallas_call(
        functools.partial(_inproj_kernel, transposed_qkv=transposed, n_aliased=len(aliases)),
        grid=(n_tok // tm,),
        in_specs=in_specs,
        out_specs=qkv_specs + rest_specs,
        out_shape=qkv_shapes + rest_shapes,
        input_output_aliases=aliases,
        compiler_params=pltpu.CompilerParams(dimension_semantics=("arbitrary",),
                                             vmem_limit_bytes=VMEM_LIMIT_BYTES),
        name="inproj",
    )(*args)


def _moba_prompt_kernel(tab_ref, qt_ref, kt_ref, vt_ref, bias_ref, o_ref,
                        kb_scr, vb_scr, kmbd_scr, neg_scr, qh_scr, s_scr, m_scr, l_scr, acc_scr):
    qt = pl.program_id(1)
    n_blk = kb_scr.shape[0]
    tq = qt_ref.shape[1]

    @pl.when(qt == 0)
    def _():
        lane_head = lax.broadcasted_iota(jnp.int32, (1, D_ATTN), 1) // HEAD_DIM
        for n in range(n_blk):
            kblk = kt_ref[:, n * MOBA_BLOCK:(n + 1) * MOBA_BLOCK].T
            kb_scr[n] = kblk.astype(BF16)
            vb_scr[n] = vt_ref[:, n * MOBA_BLOCK:(n + 1) * MOBA_BLOCK].astype(BF16)
            k_mean = jnp.sum(kblk, axis=0, keepdims=True) * (1.0 / MOBA_BLOCK)
            for h in range(N_HEADS):
                kmbd_scr[h * n_blk + n:h * n_blk + n + 1, :] = jnp.where(lane_head == h, k_mean, 0.0)

    q_t = (qt_ref[...].astype(F32) * SCALE).astype(BF16)

    def select_blocks():
        gate = jnp.dot(kmbd_scr[...].astype(BF16), q_t, preferred_element_type=F32)
        blk = lax.broadcasted_iota(jnp.int32, (n_blk, tq), 0)
        for h in range(N_HEADS):
            g = gate[h * n_blk:(h + 1) * n_blk, :]
            beaten = jnp.zeros((n_blk, tq), jnp.int32)
            for m in range(n_blk):
                gm = g[m:m + 1, :]
                wins = (gm > g) | ((gm == g) & (m < blk))
                beaten = beaten + jnp.where(wins, 1, 0) * (m < qt).astype(jnp.int32)
            selected = (blk < qt) & (beaten < MOBA_TOPK)
            far = jnp.where(blk < qt - 1, tab_ref[N_BUCKETS - 1, h], 0.0)
            neg_scr[h] = jnp.where(selected, far, NEG_INF)

    half = lax.broadcasted_iota(jnp.int32, (2 * HEAD_DIM, tq), 0) // HEAD_DIM
    for h in range(N_HEADS):
        pair = h // 2
        q_pair = q_t[pair * 2 * HEAD_DIM:(pair + 1) * 2 * HEAD_DIM, :]
        qh_scr[h] = jnp.where(half == h % 2, q_pair, jnp.zeros_like(q_pair))

    def scores(h, n):
        pair = h // 2
        k_pair = kb_scr[n, :, pair * 2 * HEAD_DIM:(pair + 1) * 2 * HEAD_DIM]
        return jnp.dot(k_pair, qh_scr[h], preferred_element_type=F32)

    def sublane_groups(a):
        return a.reshape(a.shape[0] // 8, 8, a.shape[1])

    n_group = s_scr.shape[0]
    for g in range(N_HEADS // n_group):
        heads = range(g * n_group, (g + 1) * n_group)

        def extra_bias(h, n, kind):
            if kind == "own":
                return bias_ref[h, 0]
            row = neg_scr[h, pl.ds(n, 1), :]
            return bias_ref[h, 1] + row if kind == "prev" else row

        def score_blocks(blocks):
            for h in heads:
                part = m_scr[h]
                for n, kind in blocks:
                    s = (scores(h, n) + extra_bias(h, n, kind)) * LOG2_E
                    s_scr[h % n_group, n] = s
                    part = jnp.maximum(part, jnp.max(sublane_groups(s), axis=0))
                m_scr[h] = part

        for h in heads:
            m_scr[h] = jnp.full(m_scr.shape[1:], NEG_INF, F32)
        n_far = qt - 1

        @pl.when(qt == 0)
        def _():
            score_blocks([(qt, "own")])

        @pl.when((qt >= 1) & (n_far % 2 == 0))
        def _(g=g):
            score_blocks([(qt, "own")])
            if g == 0:
                select_blocks()
            score_blocks([(qt - 1, "prev")])

        @pl.when((qt >= 1) & (n_far % 2 == 1))
        def _(g=g):
            score_blocks([(qt, "own")])
            if g == 0:
                select_blocks()
            score_blocks([(qt - 1, "prev"), (qt - 2, "far")])

        def far_body(i, carry):
            score_blocks([(2 * i, "far"), (2 * i + 1, "far")])
            return carry

        lax.fori_loop(0, jnp.maximum(n_far, 0) // 2, far_body, 0)

        m_fin = {h: jnp.max(m_scr[h], axis=0, keepdims=True) for h in heads}
        for h in heads:
            l_scr[h] = jnp.zeros(l_scr.shape[1:], F32)
            acc_scr[h * HEAD_DIM:(h + 1) * HEAD_DIM, :] = jnp.zeros((HEAD_DIM, tq), F32)

        def weight_blocks(blocks):
            for h in heads:
                rows = slice(h * HEAD_DIM, (h + 1) * HEAD_DIM)
                l_part = l_scr[h]
                acc = acc_scr[rows, :]
                for n in blocks:
                    p = jnp.exp2(s_scr[h % n_group, n] - m_fin[h])
                    l_part = l_part + jnp.sum(sublane_groups(p), axis=0)
                    v_t = vb_scr[n, rows, :]
                    acc = acc + jnp.dot(v_t, p.astype(BF16), preferred_element_type=F32)
                l_scr[h] = l_part
                acc_scr[rows, :] = acc

        n_all = qt + 1

        @pl.when(n_all % 2 == 1)
        def _():
            weight_blocks([qt])

        def weight_body(i, carry):
            weight_blocks([2 * i, 2 * i + 1])
            return carry

        lax.fori_loop(0, n_all // 2, weight_body, 0)

        for h in heads:
            rows = slice(h * HEAD_DIM, (h + 1) * HEAD_DIM)
            acc_scr[rows, :] = acc_scr[rows, :] / jnp.sum(l_scr[h], axis=0, keepdims=True)
    o_ref[...] = acc_scr[...].T.astype(o_ref.dtype)


PROMPT_HEAD_GROUP = 8


def _moba_prompt(rel_bias, q_t, k_all, v_all, layer, bias_tiles):
    b, _, s = q_t.shape
    assert s % MOBA_BLOCK == 0
    n_blk = s // MOBA_BLOCK
    tq = MOBA_BLOCK
    stack_spec = pl.BlockSpec((None, None, D_ATTN, s), lambda i, j: (layer, i, 0, 0))
    return pl.pallas_call(
        _moba_prompt_kernel,
        grid=(b, s // tq),
        in_specs=[pl.BlockSpec(memory_space=pltpu.SMEM),
                  pl.BlockSpec((None, D_ATTN, tq), lambda i, j: (i, 0, j)),
                  stack_spec, stack_spec,
                  pl.BlockSpec(bias_tiles.shape, lambda i, j: (0, 0, 0, 0))],
        out_specs=pl.BlockSpec((None, tq, D_ATTN), lambda i, j: (i, j, 0)),
        out_shape=jax.ShapeDtypeStruct((b, s, D_ATTN), BF16),
        scratch_shapes=[pltpu.VMEM((n_blk, MOBA_BLOCK, D_ATTN), BF16),
                        pltpu.VMEM((n_blk, D_ATTN, MOBA_BLOCK), BF16),
                        pltpu.VMEM((N_HEADS * n_blk, D_ATTN), F32),
                        pltpu.VMEM((N_HEADS, n_blk, tq), F32),
                        pltpu.VMEM((N_HEADS, 2 * HEAD_DIM, tq), BF16),
                        pltpu.VMEM((PROMPT_HEAD_GROUP, n_blk, MOBA_BLOCK, tq), F32),
                        pltpu.VMEM((N_HEADS, 8, tq), F32),
                        pltpu.VMEM((N_HEADS, 8, tq), F32),
                        pltpu.VMEM((D_ATTN, tq), F32)],
        compiler_params=pltpu.CompilerParams(dimension_semantics=("arbitrary", "arbitrary"),
                                             vmem_limit_bytes=VMEM_LIMIT_BYTES),
        name="moba_prompt",
    )(rel_bias, q_t, k_all, v_all, bias_tiles)


def _moba_sample_kernel(pt_ref, q_ref, kn_ref, vn_ref, sb_ref, *refs, blocks_per_step, pages_per_block, n_blk):
    n_pages = blocks_per_step * pages_per_block
    k_pages = refs[:n_pages]
    v_pages = refs[n_pages:2 * n_pages]
    o_ref = refs[2 * n_pages]
    g_scr, m_scr, l_scr, o_scr = refs[2 * n_pages + 1:]
    del pt_ref
    step = pl.program_id(1)
    t_new = kn_ref.shape[0]
    rows = q_ref.shape[0]
    page = k_pages[0].shape[-1]

    row_head = lax.broadcasted_iota(jnp.int32, (rows, D_ATTN), 0) // t_new
    lane_head = lax.broadcasted_iota(jnp.int32, (rows, D_ATTN), 1) // HEAD_DIM
    own_lanes = row_head == lane_head
    q_rows = jnp.where(own_lanes, q_ref[...] * SCALE, 0.0)
    q_bf = q_rows.astype(BF16)
    lane = lax.broadcasted_iota(jnp.int32, (rows, 128), 1)

    @pl.when(step == 0)
    def _():
        g_scr[...] = jnp.full(g_scr.shape, NEG_INF, F32)
        m_scr[...] = jnp.full(m_scr.shape, NEG_INF, F32)
        l_scr[...] = jnp.zeros(l_scr.shape, F32)

    def block_t(pages, jb):
        parts = [pages[jb * pages_per_block + jp][...].reshape(D_ATTN, page) for jp in range(pages_per_block)]
        return jnp.concatenate(parts, axis=1).astype(BF16)

    def fold_pages(a):
        return sum(a[:, jp * page:(jp + 1) * page] for jp in range(pages_per_block))

    far_bias = sb_ref[:, MOBA_BLOCK + 128:MOBA_BLOCK + 129]
    g_all, m_all, l_all = g_scr[...], m_scr[...], l_scr[...]
    blocks = [step * blocks_per_step + jb for jb in range(blocks_per_step)]
    scores = [jnp.dot(q_bf, block_t(k_pages, jb), preferred_element_type=F32)
              for jb in range(blocks_per_step)]
    probs = []
    for n, s in zip(blocks, scores):
        g_all = jnp.where(lane == n, jnp.sum(fold_pages(s), axis=1, keepdims=True), g_all)
        newest = (jnp.zeros(s.shape, jnp.int32) + n) == n_blk - 1
        s = s + jnp.where(newest, sb_ref[:, 0:MOBA_BLOCK], far_bias)
        m_blk = jnp.max(s, axis=1, keepdims=True)
        p = jnp.exp(s - m_blk)
        l_all = jnp.where(lane == n, jnp.sum(fold_pages(p), axis=1, keepdims=True), l_all)
        m_all = jnp.where(lane == n, m_blk, m_all)
        probs.append(p.astype(BF16))
    g_scr[...], m_scr[...], l_scr[...] = g_all, m_all, l_all
    for jb, (n, p) in enumerate(zip(blocks, probs)):
        o_scr[n] = lax.dot_general(p, block_t(v_pages, jb), NT_DIMS, preferred_element_type=F32)

    @pl.when(step == pl.num_programs(1) - 1)
    def _():
        g = g_all
        beaten = jnp.zeros(g.shape, jnp.int32)
        for m in range(n_blk):
            gm = g[:, m:m + 1]
            beaten = beaten + jnp.where((gm > g) | ((gm == g) & (m < lane)), 1, 0)
        selected = (beaten < MOBA_TOPK) & (lane < n_blk)
        m_sel = jnp.where(selected, m_scr[...], NEG_INF)

        s_loc = jnp.full((rows, 128), NEG_INF, F32)
        for t in range(t_new):
            col = jnp.sum(q_rows * kn_ref[t:t + 1, :], axis=1, keepdims=True)
            s_loc = jnp.where(lane == t, col, s_loc)
        s_loc = s_loc + sb_ref[:, MOBA_BLOCK:MOBA_BLOCK + 128]

        m_tot = jnp.maximum(jnp.max(m_sel, axis=1, keepdims=True), jnp.max(s_loc, axis=1, keepdims=True))
        w_blk = jnp.where(selected, jnp.exp(m_sel - m_tot), 0.0)
        p_loc = jnp.exp(s_loc - m_tot)
        denom = jnp.sum(w_blk * l_all, axis=1, keepdims=True) + jnp.sum(p_loc, axis=1, keepdims=True)
        acc = jnp.zeros((rows, D_ATTN), F32)
        for n in range(n_blk):
            acc = acc + w_blk[:, n:n + 1] * o_scr[n]
        for t in range(t_new):
            acc = acc + p_loc[:, t:t + 1] * vn_ref[t:t + 1, :]
        acc = jnp.where(own_lanes, acc / denom, 0.0)
        folded = acc
        for h in range(1, N_HEADS):
            folded = folded + pltpu.roll(acc, rows - h * t_new, 0)
        o_ref[...] = folded[:t_new, :]


def _moba_sample(q, k_new, v_new, cache_kt, cache_vt, layer, page_table, sbias, blocks_per_step):
    db, t_new, _ = q.shape
    page = cache_kt.shape[-1]
    n_pages = page_table.shape[1]
    pages_per_block = MOBA_BLOCK // page
    assert page == 128 and MOBA_BLOCK % page == 0
    assert (n_pages * page) % MOBA_BLOCK == 0 and t_new <= 128
    n_blk = n_pages // pages_per_block
    assert MOBA_TOPK <= n_blk <= 128 and n_blk % blocks_per_step == 0
    assert MOBA_BLOCK + 1 >= FAR_DISTANCE
    rows = N_HEADS * t_new
    q_rows = jnp.tile(q, (1, N_HEADS, 1))
    pages_per_step = blocks_per_step * pages_per_block

    def page_spec(j):
        return pl.BlockSpec((None, None, N_HEADS, HEAD_DIM, page),
                            lambda b, s, pt, j=j: (layer, pt[b, s * pages_per_step + j], 0, 0, 0))

    tok_spec = pl.BlockSpec((None, t_new, D_ATTN), lambda b, s, pt: (b, 0, 0))
    grid_spec = pltpu.PrefetchScalarGridSpec(
        num_scalar_prefetch=1,
        grid=(db, n_blk // blocks_per_step),
        in_specs=[pl.BlockSpec((None, rows, D_ATTN), lambda b, s, pt: (b, 0, 0)), tok_spec, tok_spec,
                  pl.BlockSpec(sbias.shape, lambda b, s, pt: (0, 0))]
                 + [page_spec(j) for j in range(pages_per_step)] * 2,
        out_specs=tok_spec,
        scratch_shapes=[pltpu.VMEM((rows, 128), F32), pltpu.VMEM((rows, 128), F32), pltpu.VMEM((rows, 128), F32),
                        pltpu.VMEM((n_blk, rows, D_ATTN), F32)],
    )
    return pl.pallas_call(
        functools.partial(_moba_sample_kernel, blocks_per_step=blocks_per_step,
                          pages_per_block=pages_per_block, n_blk=n_blk),
        grid_spec=grid_spec,
        out_shape=jax.ShapeDtypeStruct((db, t_new, D_ATTN), F32),
        compiler_params=pltpu.CompilerParams(dimension_semantics=("arbitrary", "arbitrary"),
                                             vmem_limit_bytes=VMEM_LIMIT_BYTES),
        name="moba_sample",
    )(page_table, q_rows, k_new, v_new, sbias, *([cache_kt] * pages_per_step), *([cache_vt] * pages_per_step))


def _merge_kernel(x_ref, attn_ref, ga_ref, u_ref, vn_ref, gb_ref, ma_ref, mb_ref,
                  mix_ref, sb_ref, wpa_ref, wpb_ref, wo_ref, fg_ref, o_ref, *, final):
    tm = x_ref.shape[0]
    d_sgu = u_ref.shape[-1]
    group_dim = d_sgu // N_GROUPS
    first_of_pair = lax.broadcasted_iota(jnp.int32, (CHUNK, 2 * group_dim), 1) < group_dim
    z_chunks = []
    for c in range(tm // CHUNK):
        vn = vn_ref[c * CHUNK:(c + 1) * CHUNK, :].astype(BF16)
        z_pairs = []
        for j in range(N_GROUPS // 2):
            v_pair = vn[:, j * 2 * group_dim:(j + 1) * 2 * group_dim]
            z_a = jnp.dot(mix_ref[2 * j], v_pair, preferred_element_type=F32)
            z_b = jnp.dot(mix_ref[2 * j + 1], v_pair, preferred_element_type=F32)
            z_pairs.append(jnp.where(first_of_pair, z_a, z_b))
        z_chunks.append(jnp.concatenate(z_pairs, axis=1) + sb_ref[...])
    z = jnp.concatenate(z_chunks, axis=0) if len(z_chunks) > 1 else z_chunks[0]
    y_b = (u_ref[...].astype(F32) * z) * _silu(gb_ref[...].astype(F32))
    y_a = attn_ref[...].astype(F32) * _silu(ga_ref[...].astype(F32))
    p_a = jnp.dot(y_a.astype(BF16), wpa_ref[...], preferred_element_type=F32)
    p_b = jnp.dot(y_b.astype(BF16), wpb_ref[...], preferred_element_type=F32)
    mixed = (jax.nn.sigmoid(ma_ref[...].astype(F32)) * p_a
             + jax.nn.sigmoid(mb_ref[...].astype(F32)) * p_b)
    out = x_ref[...] + jnp.dot(mixed.astype(BF16), wo_ref[...], preferred_element_type=F32)
    if final:
        out = out * lax.rsqrt(jnp.mean(out * out, axis=-1, keepdims=True) + EPS) * fg_ref[...]
    o_ref[...] = out


def _merge(x, attn, ga, u, vn, gb, ma, mb, mix_bf, sgu_bias, wpa_bf, wpb_bf, wo_bf, final_g, tm, final):
    n_tok, d_model = x.shape
    d_sgu = u.shape[-1]
    assert n_tok % tm == 0 and tm % CHUNK == 0
    row = lambda i: (i, 0)
    fixed2 = lambda i: (0, 0)
    narrow = pl.BlockSpec((tm, d_sgu), row)
    wide = pl.BlockSpec((tm, d_model), row)
    return pl.pallas_call(
        functools.partial(_merge_kernel, final=final),
        grid=(n_tok // tm,),
        in_specs=[wide, pl.BlockSpec((tm, D_ATTN), row), pl.BlockSpec((tm, D_ATTN), row),
                  narrow, narrow, narrow, wide, wide,
                  pl.BlockSpec(mix_bf.shape, lambda i: (0, 0, 0)),
                  pl.BlockSpec(sgu_bias.shape, fixed2),
                  pl.BlockSpec(wpa_bf.shape, fixed2),
                  pl.BlockSpec(wpb_bf.shape, fixed2),
                  pl.BlockSpec(wo_bf.shape, fixed2),
                  pl.BlockSpec((1, d_model), fixed2)],
        out_specs=wide,
        out_shape=jax.ShapeDtypeStruct((n_tok, d_model), F32),
        compiler_params=pltpu.CompilerParams(dimension_semantics=("arbitrary",),
                                             vmem_limit_bytes=VMEM_LIMIT_BYTES),
        name="merge",
    )(x, attn, ga, u, vn, gb, ma, mb, mix_bf, sgu_bias, wpa_bf, wpb_bf, wo_bf, final_g.reshape(1, d_model))


PROMPT_TOKEN_TILE = 256
MERGE_TOKEN_TILE = 512
SAMPLE_BLOCKS_PER_STEP = 8


def kernel(x_prompt, x_sample, cache_k, cache_v, page_table, rel_bias, norm_g, w_in, sgu_ln_g, sgu_ln_b,
           w_s, b_s, w_pa, w_pb, w_o, final_norm_g):
    b, s, d_model = x_prompt.shape
    db, t_new, _ = x_sample.shape
    depth = w_in.shape[0]
    d_sgu = sgu_ln_g.shape[-1]
    group_dim = d_sgu // N_GROUPS
    assert s % CHUNK == 0 and (db * t_new) % CHUNK == 0 and CHUNK % t_new == 0
    assert cache_k.shape[3] == N_HEADS and cache_k.shape[4] == HEAD_DIM

    bias_tiles = _prompt_bias_tiles(rel_bias)
    sbias = _sample_bias(rel_bias, t_new)
    cache_kt = jnp.transpose(cache_k, (0, 1, 3, 4, 2))
    cache_vt = jnp.transpose(cache_v, (0, 1, 3, 4, 2))

    xp = x_prompt.reshape(b * s, d_model)
    xs = x_sample.reshape(db * t_new, d_model)
    kv_stack = None
    k_s, v_s, sgu_rows = [], [], []
    for l in range(depth):
        final = l == depth - 1
        wqkv = w_in[l][:, :3 * D_ATTN]
        wqkv_bf, wqkv_t_bf = wqkv.astype(BF16), wqkv.T.astype(BF16)
        w_bf = w_in[l][:, 3 * D_ATTN:].astype(BF16)
        wpa_bf, wpb_bf, wo_bf = w_pa[l].astype(BF16), w_pb[l].astype(BF16), w_o[l].astype(BF16)
        mix_p = jnp.tril(w_s[l]).astype(BF16)
        bias_p = jnp.repeat(b_s[l].T, group_dim, axis=1)
        corner = jnp.tril(w_s[l][:, :t_new, :t_new])
        eye = jnp.eye(CHUNK // t_new, dtype=F32)
        mix_s = jnp.einsum("ab,gts->gatbs", eye, corner).reshape(N_GROUPS, CHUNK, CHUNK).astype(BF16)
        bias_s = jnp.tile(jnp.repeat(b_s[l][:, :t_new].T, group_dim, axis=1), (CHUNK // t_new, 1))

        q_t, k_all, v_all, ga, u, vn, gb, ma, mb = _inproj(
            xp, norm_g[l], wqkv_t_bf, w_bf, sgu_ln_g[l], sgu_ln_b[l], PROMPT_TOKEN_TILE,
            stack_shape=(depth, b, D_ATTN, s), kv_stack=kv_stack, layer=l, seq_len=s)
        kv_stack = (k_all, v_all)
        attn = _moba_prompt(rel_bias, q_t, k_all, v_all, l, bias_tiles)
        xp = _merge(xp, attn.reshape(b * s, D_ATTN), ga, u, vn, gb, ma, mb, mix_p, bias_p,
                    wpa_bf, wpb_bf, wo_bf, final_norm_g, MERGE_TOKEN_TILE, final)

        q, k, v, ga, u, vn, gb, ma, mb = _inproj(xs, norm_g[l], wqkv_bf, w_bf, sgu_ln_g[l], sgu_ln_b[l], CHUNK)
        attn = _moba_sample(q.reshape(db, t_new, D_ATTN), k.reshape(db, t_new, D_ATTN), v.reshape(db, t_new, D_ATTN),
                            cache_kt, cache_vt, l, page_table, sbias, SAMPLE_BLOCKS_PER_STEP)
        xs = _merge(xs, attn.reshape(db * t_new, D_ATTN), ga, u, vn, gb, ma, mb, mix_s, bias_s,
                    wpa_bf, wpb_bf, wo_bf, final_norm_g, CHUNK, final)
        k_s.append(k.reshape(db, t_new, N_HEADS, HEAD_DIM))
        v_s.append(v.reshape(db, t_new, N_HEADS, HEAD_DIM))
        sgu_rows.append(vn.reshape(db, t_new, d_sgu))

    k_prompt = jnp.transpose(k_all.reshape(depth, b, N_HEADS, HEAD_DIM, s), (0, 1, 4, 2, 3))
    v_prompt = jnp.transpose(v_all.reshape(depth, b, N_HEADS, HEAD_DIM, s), (0, 1, 4, 2, 3))
    return (xp.reshape(b, s, d_model), xs.reshape(db, t_new, d_model),
            k_prompt, v_prompt, jnp.stack(k_s), jnp.stack(v_s), jnp.stack(sgu_rows))
```

```python
import functools
import math

import numpy as np
import jax
import jax.numpy as jnp
from jax import lax
from jax.experimental import pallas as pl
from jax.experimental.pallas import tpu as pltpu

N_HEADS = 8
HEAD_DIM = 64
D_ATTN = N_HEADS * HEAD_DIM
MOBA_BLOCK = 256
MOBA_TOPK = 3
N_GROUPS = 8
CHUNK = 128
N_BUCKETS = 32
MAX_DISTANCE = 128
EPS = 1e-6
LN_EPS = 1e-5
SCALE = HEAD_DIM ** -0.5
LOG2_E = math.log2(math.e)

F32 = jnp.float32
BF16 = jnp.bfloat16
NEG_INF = float("-inf")
NT_DIMS = (((1,), (1,)), ((), ()))

VMEM_LIMIT_BYTES = 56 * 1024 * 1024


def _bucket_thresholds():
    max_exact = N_BUCKETS // 2
    d = np.arange(max_exact, 4 * MAX_DISTANCE, dtype=np.float64)
    log_b = max_exact + (np.log(d / max_exact) / math.log(MAX_DISTANCE / max_exact) * (N_BUCKETS - max_exact)).astype(np.int64)
    bucket = np.minimum(log_b, N_BUCKETS - 1)
    return [int(d[np.argmax(bucket >= b)]) for b in range(max_exact + 1, N_BUCKETS)]


BUCKET_THRESHOLDS = _bucket_thresholds()
FAR_DISTANCE = BUCKET_THRESHOLDS[-1]


def _rel_bucket(dist):
    max_exact = N_BUCKETS // 2
    log_b = jnp.full(dist.shape, max_exact, jnp.int32)
    for thr in BUCKET_THRESHOLDS:
        log_b = log_b + jnp.where(dist >= thr, 1, 0)
    return jnp.where(dist < max_exact, dist, log_b)


def _gelu_tanh(x):
    return 0.5 * x * (1.0 + jnp.tanh(math.sqrt(2.0 / math.pi) * (x + 0.044715 * (x * x * x))))


def _silu(x):
    return x * jax.nn.sigmoid(x)


def _prompt_bias_kernel(tab_ref, o_ref):
    h = pl.program_id(0)
    c = pl.program_id(1)
    shape = o_ref.shape
    key = lax.broadcasted_iota(jnp.int32, shape, 0)
    qry = lax.broadcasted_iota(jnp.int32, shape, 1)
    dist = c * MOBA_BLOCK + qry - key
    bucket = _rel_bucket(dist)
    val = jnp.full(shape, tab_ref[N_BUCKETS - 1, h], F32)
    for b in range(N_BUCKETS - 1):
        val = jnp.where(bucket == b, tab_ref[b, h], val)
    o_ref[...] = jnp.where(dist < 0, NEG_INF, val)


def _prompt_bias_tiles(rel_bias):
    assert MOBA_BLOCK + 1 >= FAR_DISTANCE
    return pl.pallas_call(
        _prompt_bias_kernel,
        grid=(N_HEADS, 2),
        in_specs=[pl.BlockSpec(memory_space=pltpu.SMEM)],
        out_specs=pl.BlockSpec((None, None, MOBA_BLOCK, MOBA_BLOCK), lambda h, c: (h, c, 0, 0)),
        out_shape=jax.ShapeDtypeStruct((N_HEADS, 2, MOBA_BLOCK, MOBA_BLOCK), F32),
        compiler_params=pltpu.CompilerParams(dimension_semantics=("arbitrary", "arbitrary")),
        name="prompt_bias_tiles",
    )(rel_bias)


def _sample_bias_kernel(tab_ref, o_ref, *, t_new):
    shape = o_ref.shape
    row = lax.broadcasted_iota(jnp.int32, shape, 0)
    lane = lax.broadcasted_iota(jnp.int32, shape, 1)
    t = row % t_new
    dist = jnp.where(lane < MOBA_BLOCK, MOBA_BLOCK + t - lane,
                     jnp.where(lane < MOBA_BLOCK + 128, t - (lane - MOBA_BLOCK), FAR_DISTANCE))
    bucket = _rel_bucket(dist)
    tab = tab_ref[...]
    val = jnp.broadcast_to(tab[:, N_BUCKETS - 1:N_BUCKETS], shape)
    for b in range(N_BUCKETS - 1):
        val = jnp.where(bucket == b, tab[:, b:b + 1], val)
    o_ref[...] = jnp.where(dist < 0, NEG_INF, val)


def _sample_bias(rel_bias, t_new):
    rows = N_HEADS * t_new
    tab_rows = jnp.repeat(rel_bias.T, t_new, axis=0)
    return pl.pallas_call(
        functools.partial(_sample_bias_kernel, t_new=t_new),
        out_shape=jax.ShapeDtypeStruct((rows, MOBA_BLOCK + 256), F32),
        name="sample_bias",
    )(tab_rows)


def _inproj_kernel(*refs, transposed_qkv, n_aliased):
    x_ref, g_ref, wqkv_ref, w_ref, lng_ref, lnb_ref = refs[:6]
    q_ref, k_ref, v_ref, ga_ref, u_ref, vn_ref, gb_ref, ma_ref, mb_ref = refs[6 + n_aliased:6 + n_aliased + 9]
    x = x_ref[...]
    d_model = x.shape[-1]
    d_sgu = u_ref.shape[-1]
    h = (x * lax.rsqrt(jnp.mean(x * x, axis=-1, keepdims=True) + EPS) * g_ref[...]).astype(BF16)

    if transposed_qkv:
        qkv_t = lax.dot_general(wqkv_ref[...], h, NT_DIMS, preferred_element_type=F32)
        q_ref[...] = qkv_t[0:D_ATTN].astype(q_ref.dtype)
        for ref, rows in ((k_ref, qkv_t[D_ATTN:2 * D_ATTN]), (v_ref, qkv_t[2 * D_ATTN:3 * D_ATTN])):
            if len(ref.shape) == 2:
                ref[...] = rows
            else:
                ref[0] = rows
                for later in range(1, ref.shape[0]):
                    ref[later] = jnp.zeros_like(rows)
        kb_ref, vb_ref, km_ref = refs[6 + n_aliased + 9:]
        k_rows = qkv_t[D_ATTN:2 * D_ATTN].T
        kb_ref[...] = k_rows.astype(BF16)
        vb_ref[...] = qkv_t[2 * D_ATTN:3 * D_ATTN].astype(BF16)
        km_ref[...] = jnp.sum(k_rows, axis=0, keepdims=True) * (1.0 / k_rows.shape[0])
    else:
        qkv = jnp.dot(h, wqkv_ref[...], preferred_element_type=F32)
        q_ref[...] = qkv[:, 0:D_ATTN]
        k_ref[...] = qkv[:, D_ATTN:2 * D_ATTN]
        v_ref[...] = qkv[:, 2 * D_ATTN:3 * D_ATTN]

    def seg(lo, width):
        return jnp.dot(h, w_ref[:, lo:lo + width], preferred_element_type=F32)

    act = ga_ref.dtype
    ga_ref[...] = seg(0, D_ATTN).astype(act)
    base = D_ATTN
    u_ref[...] = _gelu_tanh(seg(base, d_sgu)).astype(act)
    vs = _gelu_tanh(seg(base + d_sgu, d_sgu))
    mu = jnp.mean(vs, axis=-1, keepdims=True)
    var = jnp.mean(jnp.square(vs - mu), axis=-1, keepdims=True)
    vn_ref[...] = ((vs - mu) * lax.rsqrt(var + LN_EPS) * lng_ref[...] + lnb_ref[...]).astype(act)
    gb_ref[...] = seg(base + 2 * d_sgu, d_sgu).astype(act)
    ma_ref[...] = seg(base + 3 * d_sgu, d_model).astype(act)
    mb_ref[...] = seg(base + 3 * d_sgu + d_model, d_model).astype(act)


def _inproj(x, norm_g, wqkv_bf, w_bf, ln_g, ln_b, tm, stack_shape=None, kv_stack=None, layer=None, seq_len=None):
    n_tok, d_model = x.shape
    d_sgu = ln_g.shape[-1]
    assert n_tok % tm == 0
    widths = [D_ATTN] + [d_sgu] * 3 + [d_model] * 2
    assert sum(widths) == w_bf.shape[1]
    row = lambda i: (i, 0)
    fixed = lambda i: (0, 0)
    transposed = stack_shape is not None
    in_specs = [pl.BlockSpec((tm, d_model), row),
                pl.BlockSpec((1, d_model), fixed),
                pl.BlockSpec(wqkv_bf.shape, fixed),
                pl.BlockSpec(w_bf.shape, fixed),
                pl.BlockSpec((1, d_sgu), fixed),
                pl.BlockSpec((1, d_sgu), fixed)]
    args = [x, norm_g.reshape(1, d_model), wqkv_bf, w_bf, ln_g.reshape(1, d_sgu), ln_b.reshape(1, d_sgu)]
    rest_specs = [pl.BlockSpec((tm, w), row) for w in widths]
    act_dtype = BF16 if transposed else F32
    rest_shapes = [jax.ShapeDtypeStruct((n_tok, w), act_dtype) for w in widths]
    aliases = {}
    if transposed:
        assert seq_len % tm == 0 and n_tok % seq_len == 0
        tiles = seq_len // tm
        batch = n_tok // seq_len
        if kv_stack is not None:
            in_specs += [pl.BlockSpec(memory_space=pl.ANY)] * 2
            args += list(kv_stack)
            aliases = {6: 1, 7: 2}
            stack_spec = pl.BlockSpec((None, None, D_ATTN, tm), lambda i: (layer, i // tiles, 0, i % tiles))
        else:
            assert layer == 0
            stack_spec = pl.BlockSpec((stack_shape[0], None, D_ATTN, tm), lambda i: (0, i // tiles, 0, i % tiles))
        qkv_specs = [pl.BlockSpec((None, D_ATTN, tm), lambda i: (i // tiles, 0, i % tiles)), stack_spec, stack_spec]
        qkv_shapes = [jax.ShapeDtypeStruct((batch, D_ATTN, seq_len), BF16),
                      jax.ShapeDtypeStruct(stack_shape, F32), jax.ShapeDtypeStruct(stack_shape, F32)]
    else:
        qkv_specs = [pl.BlockSpec((tm, D_ATTN), row)] * 3
        qkv_shapes = [jax.ShapeDtypeStruct((n_tok, D_ATTN), F32)] * 3
    block_specs, block_shapes = [], []
    if transposed:
        assert tm == MOBA_BLOCK
        blk_map = lambda i: (i // tiles, i % tiles, 0, 0)
        block_specs = [pl.BlockSpec((None, None, tm, D_ATTN), blk_map),
                       pl.BlockSpec((None, None, D_ATTN, tm), blk_map),
                       pl.BlockSpec((None, None, 1, D_ATTN), blk_map)]
        block_shapes = [jax.ShapeDtypeStruct((batch, tiles, tm, D_ATTN), BF16),
                        jax.ShapeDtypeStruct((batch, tiles, D_ATTN, tm), BF16),
                        jax.ShapeDtypeStruct((batch, tiles, 1, D_ATTN), F32)]
    return pl.pallas_call(
        functools.partial(_inproj_kernel, transposed_qkv=transposed, n_aliased=len(aliases)),
        grid=(n_tok // tm,),
        in_specs=in_specs,
        out_specs=qkv_specs + rest_specs + block_specs,
        out_shape=qkv_shapes + rest_shapes + block_shapes,
        input_output_aliases=aliases,
        compiler_params=pltpu.CompilerParams(dimension_semantics=("arbitrary",),
                                             vmem_limit_bytes=VMEM_LIMIT_BYTES),
        name="inproj",
    )(*args)


def _prompt_attention(qt, sub, tab_ref, qt_ref, kb_scr, vb_scr, km_ref, bias_ref, o_ref,
                      kmbd_scr, neg_scr, qh_scr, s_scr, m_scr, l_scr, acc_scr):
    n_blk = kb_scr.shape[0]
    tq = qt_ref.shape[1]
    scoring = sub == 0
    weighting = sub == 1

    @pl.when(scoring & (qt == 0))
    def _():
        lane_head = lax.broadcasted_iota(jnp.int32, (1, D_ATTN), 1) // HEAD_DIM
        for n in range(n_blk):
            k_mean = km_ref[n]
            for h in range(N_HEADS):
                kmbd_scr[h * n_blk + n:h * n_blk + n + 1, :] = jnp.where(lane_head == h, k_mean, 0.0)

    def scaled_q():
        return (qt_ref[...].astype(F32) * SCALE).astype(BF16)

    def select_blocks():
        gate = jnp.dot(kmbd_scr[...].astype(BF16), scaled_q(), preferred_element_type=F32)
        blk = lax.broadcasted_iota(jnp.int32, (n_blk, tq), 0)
        for h in range(N_HEADS):
            g = gate[h * n_blk:(h + 1) * n_blk, :]
            beaten = jnp.zeros((n_blk, tq), jnp.int32)
            for m in range(n_blk):
                gm = g[m:m + 1, :]
                wins = (gm > g) | ((gm == g) & (m < blk))
                beaten = beaten + jnp.where(wins, 1, 0) * (m < qt).astype(jnp.int32)
            selected = (blk < qt) & (beaten < MOBA_TOPK)
            far = jnp.where(blk < qt - 1, tab_ref[N_BUCKETS - 1, h], 0.0)
            neg_scr[h] = jnp.where(selected, far, NEG_INF)

    @pl.when(scoring)
    def _():
        q_t = scaled_q()
        half = lax.broadcasted_iota(jnp.int32, (2 * HEAD_DIM, tq), 0) // HEAD_DIM
        for h in range(N_HEADS):
            pair = h // 2
            q_pair = q_t[pair * 2 * HEAD_DIM:(pair + 1) * 2 * HEAD_DIM, :]
            qh_scr[h] = jnp.where(half == h % 2, q_pair, jnp.zeros_like(q_pair))
            m_scr[h] = jnp.full(m_scr.shape[1:], NEG_INF, F32)

    def scores(h, n):
        pair = h // 2
        k_pair = kb_scr[n, :, pair * 2 * HEAD_DIM:(pair + 1) * 2 * HEAD_DIM]
        return jnp.dot(k_pair, qh_scr[h], preferred_element_type=F32)

    def sublane_groups(a):
        return a.reshape(a.shape[0] // 8, 8, a.shape[1])

    n_group = s_scr.shape[0]
    for g in range(N_HEADS // n_group):
        heads = range(g * n_group, (g + 1) * n_group)

        def extra_bias(h, n, kind):
            if kind == "own":
                return bias_ref[h, 0]
            row = neg_scr[h, pl.ds(n, 1), :]
            return bias_ref[h, 1] + row if kind == "prev" else row

        def score_blocks(blocks):
            for h in heads:
                part = m_scr[h]
                for n, kind in blocks:
                    s = (scores(h, n) + extra_bias(h, n, kind)) * LOG2_E
                    s_scr[h % n_group, n] = s
                    part = jnp.maximum(part, jnp.max(sublane_groups(s), axis=0))
                m_scr[h] = part

        n_far = qt - 1

        @pl.when(scoring & (qt == 0))
        def _():
            score_blocks([(qt, "own")])

        @pl.when(scoring & (qt >= 1) & (n_far % 2 == 0))
        def _(g=g):
            score_blocks([(qt, "own")])
            if g == 0:
                select_blocks()
            score_blocks([(qt - 1, "prev")])

        @pl.when(scoring & (qt >= 1) & (n_far % 2 == 1))
        def _(g=g):
            score_blocks([(qt, "own")])
            if g == 0:
                select_blocks()
            score_blocks([(qt - 1, "prev"), (qt - 2, "far")])

        def far_body(i, carry):
            score_blocks([(2 * i, "far"), (2 * i + 1, "far")])
            return carry

        lax.fori_loop(0, jnp.where(scoring, jnp.maximum(n_far, 0) // 2, 0), far_body, 0)

        m_fin = {h: jnp.max(m_scr[h], axis=0, keepdims=True) for h in heads}

        @pl.when(weighting)
        def _():
            for h in heads:
                l_scr[h] = jnp.zeros(l_scr.shape[1:], F32)
                acc_scr[h * HEAD_DIM:(h + 1) * HEAD_DIM, :] = jnp.zeros((HEAD_DIM, tq), F32)

        def weight_blocks(blocks):
            for h in heads:
                rows = slice(h * HEAD_DIM, (h + 1) * HEAD_DIM)
                l_part = l_scr[h]
                acc = acc_scr[rows, :]
                for n in blocks:
                    p = jnp.exp2(s_scr[h % n_group, n] - m_fin[h])
                    l_part = l_part + jnp.sum(sublane_groups(p), axis=0)
                    v_t = vb_scr[n, rows, :]
                    acc = acc + jnp.dot(v_t, p.astype(BF16), preferred_element_type=F32)
                l_scr[h] = l_part
                acc_scr[rows, :] = acc

        n_all = qt + 1

        @pl.when(weighting & (n_all % 2 == 1))
        def _():
            weight_blocks([qt])

        def weight_body(i, carry):
            weight_blocks([2 * i, 2 * i + 1])
            return carry

        lax.fori_loop(0, jnp.where(weighting, n_all // 2, 0), weight_body, 0)

        @pl.when(weighting)
        def _():
            for h in heads:
                rows = slice(h * HEAD_DIM, (h + 1) * HEAD_DIM)
                acc_scr[rows, :] = acc_scr[rows, :] / jnp.sum(l_scr[h], axis=0, keepdims=True)

    @pl.when(weighting)
    def _():
        o_ref[...] = acc_scr[...].T.astype(o_ref.dtype)


def _sample_attention(step, n_steps, q_ref, kn_ref, vn_ref, sb_ref, k_pages, v_pages, o_ref,
                      g_scr, m_scr, l_scr, o_scr, *, pages_per_block, n_blk):
    blocks_per_step = len(k_pages) // pages_per_block
    t_new = kn_ref.shape[0]
    rows = q_ref.shape[0]
    page = k_pages[0].shape[-1]

    row_head = lax.broadcasted_iota(jnp.int32, (rows, D_ATTN), 0) // t_new
    lane_head = lax.broadcasted_iota(jnp.int32, (rows, D_ATTN), 1) // HEAD_DIM
    own_lanes = row_head == lane_head
    q_rows = jnp.where(own_lanes, q_ref[...] * SCALE, 0.0)
    q_bf = q_rows.astype(BF16)
    lane = lax.broadcasted_iota(jnp.int32, (rows, 128), 1)

    @pl.when(step == 0)
    def _():
        g_scr[...] = jnp.full(g_scr.shape, NEG_INF, F32)
        m_scr[...] = jnp.full(m_scr.shape, NEG_INF, F32)
        l_scr[...] = jnp.zeros(l_scr.shape, F32)

    def block_t(pages, jb):
        parts = [pages[jb * pages_per_block + jp][...].reshape(D_ATTN, page) for jp in range(pages_per_block)]
        return jnp.concatenate(parts, axis=1).astype(BF16)

    def fold_pages(a):
        return sum(a[:, jp * page:(jp + 1) * page] for jp in range(pages_per_block))

    far_bias = sb_ref[:, MOBA_BLOCK + 128:MOBA_BLOCK + 129]
    g_all, m_all, l_all = g_scr[...], m_scr[...], l_scr[...]
    blocks = [step * blocks_per_step + jb for jb in range(blocks_per_step)]
    scores = [jnp.dot(q_bf, block_t(k_pages, jb), preferred_element_type=F32)
              for jb in range(blocks_per_step)]
    probs = []
    for n, s in zip(blocks, scores):
        g_all = jnp.where(lane == n, jnp.sum(fold_pages(s), axis=1, keepdims=True), g_all)
        newest = (jnp.zeros(s.shape, jnp.int32) + n) == n_blk - 1
        s = s + jnp.where(newest, sb_ref[:, 0:MOBA_BLOCK], far_bias)
        m_blk = jnp.max(s, axis=1, keepdims=True)
        p = jnp.exp(s - m_blk)
        l_all = jnp.where(lane == n, jnp.sum(fold_pages(p), axis=1, keepdims=True), l_all)
        m_all = jnp.where(lane == n, m_blk, m_all)
        probs.append(p.astype(BF16))
    g_scr[...], m_scr[...], l_scr[...] = g_all, m_all, l_all
    for jb, (n, p) in enumerate(zip(blocks, probs)):
        o_scr[n] = lax.dot_general(p, block_t(v_pages, jb), NT_DIMS, preferred_element_type=F32)

    @pl.when(step == n_steps - 1)
    def _():
        g = g_all
        beaten = jnp.zeros(g.shape, jnp.int32)
        for m in range(n_blk):
            gm = g[:, m:m + 1]
            beaten = beaten + jnp.where((gm > g) | ((gm == g) & (m < lane)), 1, 0)
        selected = (beaten < MOBA_TOPK) & (lane < n_blk)
        m_sel = jnp.where(selected, m_scr[...], NEG_INF)

        s_loc = jnp.full((rows, 128), NEG_INF, F32)
        for t in range(t_new):
            col = jnp.sum(q_rows * kn_ref[t:t + 1, :], axis=1, keepdims=True)
            s_loc = jnp.where(lane == t, col, s_loc)
        s_loc = s_loc + sb_ref[:, MOBA_BLOCK:MOBA_BLOCK + 128]

        m_tot = jnp.maximum(jnp.max(m_sel, axis=1, keepdims=True), jnp.max(s_loc, axis=1, keepdims=True))
        w_blk = jnp.where(selected, jnp.exp(m_sel - m_tot), 0.0)
        p_loc = jnp.exp(s_loc - m_tot)
        denom = jnp.sum(w_blk * l_all, axis=1, keepdims=True) + jnp.sum(p_loc, axis=1, keepdims=True)
        acc = jnp.zeros((rows, D_ATTN), F32)
        for n in range(n_blk):
            acc = acc + w_blk[:, n:n + 1] * o_scr[n]
        for t in range(t_new):
            acc = acc + p_loc[:, t:t + 1] * vn_ref[t:t + 1, :]
        acc = jnp.where(own_lanes, acc / denom, 0.0)
        folded = acc
        for h in range(1, N_HEADS):
            folded = folded + pltpu.roll(acc, rows - h * t_new, 0)
        o_ref[...] = folded[:t_new, :]


def _attention_kernel(pt_ref, tab_ref, qt_ref, kb_ref, vb_ref, km_ref, bias_ref, sq_ref, skn_ref, svn_ref, sb_ref,
                      *refs, n_windows, pages_per_block, n_blk_sample, steps_per_seq):
    del pt_ref
    k_pages, v_pages = refs[:n_windows], refs[n_windows:2 * n_windows]
    o_prompt, o_sample = refs[2 * n_windows:2 * n_windows + 2]
    prompt_scratch = refs[2 * n_windows + 2:2 * n_windows + 9]
    sample_scratch = refs[2 * n_windows + 9:]
    qt = pl.program_id(1)
    sub = pl.program_id(2)
    linear = (pl.program_id(0) * pl.num_programs(1) + qt) * pl.num_programs(2) + sub
    _sample_attention(linear % steps_per_seq, steps_per_seq, sq_ref, skn_ref, svn_ref, sb_ref, k_pages, v_pages,
                      o_sample, *sample_scratch, pages_per_block=pages_per_block, n_blk=n_blk_sample)
    _prompt_attention(qt, sub, tab_ref, qt_ref, kb_ref, vb_ref, km_ref, bias_ref, o_prompt, *prompt_scratch)


def _attention(rel_bias, q_t, k_blocks, vt_blocks, k_means, bias_tiles,
               q, k_new, v_new, cache_kt, cache_vt, layer, page_table, sbias, blocks_per_step):
    b, _, s = q_t.shape
    assert s % MOBA_BLOCK == 0
    n_blk = s // MOBA_BLOCK
    tq = MOBA_BLOCK
    n_qt = s // tq
    n_pass = 2

    db, t_new, _ = q.shape
    page = cache_kt.shape[-1]
    n_pages = page_table.shape[1]
    pages_per_block = MOBA_BLOCK // page
    assert page == 128 and MOBA_BLOCK % page == 0
    assert (n_pages * page) % MOBA_BLOCK == 0 and t_new <= 128
    n_blk_s = n_pages // pages_per_block
    assert MOBA_TOPK <= n_blk_s <= 128 and n_blk_s % blocks_per_step == 0
    assert MOBA_BLOCK + 1 >= FAR_DISTANCE
    steps_per_seq = n_blk_s // blocks_per_step
    assert b * n_qt * n_pass == db * steps_per_seq
    rows = N_HEADS * t_new
    q_rows = jnp.tile(q, (1, N_HEADS, 1))
    pages_per_step = blocks_per_step * pages_per_block

    def linear(i, j, k):
        return (i * n_qt + j) * n_pass + k

    def page_spec(w):
        def index_map(i, j, k, pt):
            lin = linear(i, j, k)
            return (layer, pt[lin // steps_per_seq, (lin % steps_per_seq) * pages_per_step + w], 0, 0, 0)
        return pl.BlockSpec((None, None, N_HEADS, HEAD_DIM, page), index_map)

    per_batch = lambda i, j, k, pt: (i, 0, 0, 0)
    per_seq = lambda i, j, k, pt: (linear(i, j, k) // steps_per_seq, 0, 0)
    tok_spec = pl.BlockSpec((None, t_new, D_ATTN), per_seq)
    grid_spec = pltpu.PrefetchScalarGridSpec(
        num_scalar_prefetch=1,
        grid=(b, n_qt, n_pass),
        in_specs=[pl.BlockSpec(memory_space=pltpu.SMEM),
                  pl.BlockSpec((None, D_ATTN, tq), lambda i, j, k, pt: (i, 0, j)),
                  pl.BlockSpec((None, n_blk, MOBA_BLOCK, D_ATTN), per_batch),
                  pl.BlockSpec((None, n_blk, D_ATTN, MOBA_BLOCK), per_batch),
                  pl.BlockSpec((None, n_blk, 1, D_ATTN), per_batch),
                  pl.BlockSpec(bias_tiles.shape, lambda i, j, k, pt: (0, 0, 0, 0)),
                  pl.BlockSpec((None, rows, D_ATTN), per_seq), tok_spec, tok_spec,
                  pl.BlockSpec(sbias.shape, lambda i, j, k, pt: (0, 0))]
                 + [page_spec(w) for w in range(pages_per_step)] * 2,
        out_specs=[pl.BlockSpec((None, tq, D_ATTN), lambda i, j, k, pt: (i, j, 0)), tok_spec],
        scratch_shapes=[pltpu.VMEM((N_HEADS * n_blk, D_ATTN), F32),
                        pltpu.VMEM((N_HEADS, n_blk, tq), F32),
                        pltpu.VMEM((N_HEADS, 2 * HEAD_DIM, tq), BF16),
                        pltpu.VMEM((N_HEADS, n_blk, MOBA_BLOCK, tq), F32),
                        pltpu.VMEM((N_HEADS, 8, tq), F32),
                        pltpu.VMEM((N_HEADS, 8, tq), F32),
                        pltpu.VMEM((D_ATTN, tq), F32),
                        pltpu.VMEM((rows, 128), F32), pltpu.VMEM((rows, 128), F32), pltpu.VMEM((rows, 128), F32),
                        pltpu.VMEM((n_blk_s, rows, D_ATTN), F32)],
    )
    return pl.pallas_call(
        functools.partial(_attention_kernel, n_windows=pages_per_step, pages_per_block=pages_per_block,
                          n_blk_sample=n_blk_s, steps_per_seq=steps_per_seq),
        grid_spec=grid_spec,
        out_shape=[jax.ShapeDtypeStruct((b, s, D_ATTN), BF16), jax.ShapeDtypeStruct((db, t_new, D_ATTN), F32)],
        compiler_params=pltpu.CompilerParams(dimension_semantics=("arbitrary", "arbitrary", "arbitrary"),
                                             vmem_limit_bytes=VMEM_LIMIT_BYTES),
        name="moba_attention",
    )(page_table, rel_bias, q_t, k_blocks, vt_blocks, k_means, bias_tiles, q_rows, k_new, v_new, sbias,
      *([cache_kt] * pages_per_step), *([cache_vt] * pages_per_step))


def _merge_kernel(x_ref, attn_ref, ga_ref, u_ref, vn_ref, gb_ref, ma_ref, mb_ref,
                  mix_ref, sb_ref, wpa_ref, wpb_ref, wo_ref, fg_ref, o_ref, *, final):
    tm = x_ref.shape[0]
    d_sgu = u_ref.shape[-1]
    group_dim = d_sgu // N_GROUPS
    first_of_pair = lax.broadcasted_iota(jnp.int32, (CHUNK, 2 * group_dim), 1) < group_dim
    z_chunks = []
    for c in range(tm // CHUNK):
        vn = vn_ref[c * CHUNK:(c + 1) * CHUNK, :].astype(BF16)
        z_pairs = []
        for j in range(N_GROUPS // 2):
            v_pair = vn[:, j * 2 * group_dim:(j + 1) * 2 * group_dim]
            z_a = jnp.dot(mix_ref[2 * j], v_pair, preferred_element_type=F32)
            z_b = jnp.dot(mix_ref[2 * j + 1], v_pair, preferred_element_type=F32)
            z_pairs.append(jnp.where(first_of_pair, z_a, z_b))
        z_chunks.append(jnp.concatenate(z_pairs, axis=1) + sb_ref[...])
    z = jnp.concatenate(z_chunks, axis=0) if len(z_chunks) > 1 else z_chunks[0]
    y_b = (u_ref[...].astype(F32) * z) * _silu(gb_ref[...].astype(F32))
    y_a = attn_ref[...].astype(F32) * _silu(ga_ref[...].astype(F32))
    p_a = jnp.dot(y_a.astype(BF16), wpa_ref[...], preferred_element_type=F32)
    p_b = jnp.dot(y_b.astype(BF16), wpb_ref[...], preferred_element_type=F32)
    mixed = (jax.nn.sigmoid(ma_ref[...].astype(F32)) * p_a
             + jax.nn.sigmoid(mb_ref[...].astype(F32)) * p_b)
    out = x_ref[...] + jnp.dot(mixed.astype(BF16), wo_ref[...], preferred_element_type=F32)
    if final:
        out = out * lax.rsqrt(jnp.mean(out * out, axis=-1, keepdims=True) + EPS) * fg_ref[...]
    o_ref[...] = out


def _merge(x, attn, ga, u, vn, gb, ma, mb, mix_bf, sgu_bias, wpa_bf, wpb_bf, wo_bf, final_g, tm, final):
    n_tok, d_model = x.shape
    d_sgu = u.shape[-1]
    assert n_tok % tm == 0 and tm % CHUNK == 0
    row = lambda i: (i, 0)
    fixed2 = lambda i: (0, 0)
    narrow = pl.BlockSpec((tm, d_sgu), row)
    wide = pl.BlockSpec((tm, d_model), row)
    return pl.pallas_call(
        functools.partial(_merge_kernel, final=final),
        grid=(n_tok // tm,),
        in_specs=[wide, pl.BlockSpec((tm, D_ATTN), row), pl.BlockSpec((tm, D_ATTN), row),
                  narrow, narrow, narrow, wide, wide,
                  pl.BlockSpec(mix_bf.shape, lambda i: (0, 0, 0)),
                  pl.BlockSpec(sgu_bias.shape, fixed2),
                  pl.BlockSpec(wpa_bf.shape, fixed2),
                  pl.BlockSpec(wpb_bf.shape, fixed2),
                  pl.BlockSpec(wo_bf.shape, fixed2),
                  pl.BlockSpec((1, d_model), fixed2)],
        out_specs=wide,
        out_shape=jax.ShapeDtypeStruct((n_tok, d_model), F32),
        compiler_params=pltpu.CompilerParams(dimension_semantics=("arbitrary",),
                                             vmem_limit_bytes=VMEM_LIMIT_BYTES),
        name="merge",
    )(x, attn, ga, u, vn, gb, ma, mb, mix_bf, sgu_bias, wpa_bf, wpb_bf, wo_bf, final_g.reshape(1, d_model))


PROMPT_TOKEN_TILE = 256
MERGE_TOKEN_TILE = 512
SAMPLE_BLOCKS_PER_STEP = 8


def kernel(x_prompt, x_sample, cache_k, cache_v, page_table, rel_bias, norm_g, w_in, sgu_ln_g, sgu_ln_b,
           w_s, b_s, w_pa, w_pb, w_o, final_norm_g):
    b, s, d_model = x_prompt.shape
    db, t_new, _ = x_sample.shape
    depth = w_in.shape[0]
    d_sgu = sgu_ln_g.shape[-1]
    group_dim = d_sgu // N_GROUPS
    assert s % CHUNK == 0 and (db * t_new) % CHUNK == 0 and CHUNK % t_new == 0
    assert cache_k.shape[3] == N_HEADS and cache_k.shape[4] == HEAD_DIM

    bias_tiles = _prompt_bias_tiles(rel_bias)
    sbias = _sample_bias(rel_bias, t_new)
    cache_kt = jnp.transpose(cache_k, (0, 1, 3, 4, 2))
    cache_vt = jnp.transpose(cache_v, (0, 1, 3, 4, 2))

    xp = x_prompt.reshape(b * s, d_model)
    xs = x_sample.reshape(db * t_new, d_model)
    kv_stack = None
    k_s, v_s, sgu_rows = [], [], []
    for l in range(depth):
        final = l == depth - 1
        wqkv = w_in[l][:, :3 * D_ATTN]
        wqkv_bf, wqkv_t_bf = wqkv.astype(BF16), wqkv.T.astype(BF16)
        w_bf = w_in[l][:, 3 * D_ATTN:].astype(BF16)
        wpa_bf, wpb_bf, wo_bf = w_pa[l].astype(BF16), w_pb[l].astype(BF16), w_o[l].astype(BF16)
        mix_p = jnp.tril(w_s[l]).astype(BF16)
        bias_p = jnp.repeat(b_s[l].T, group_dim, axis=1)
        corner = jnp.tril(w_s[l][:, :t_new, :t_new])
        eye = jnp.eye(CHUNK // t_new, dtype=F32)
        mix_s = jnp.einsum("ab,gts->gatbs", eye, corner).reshape(N_GROUPS, CHUNK, CHUNK).astype(BF16)
        bias_s = jnp.tile(jnp.repeat(b_s[l][:, :t_new].T, group_dim, axis=1), (CHUNK // t_new, 1))

        q_t, k_all, v_all, ga, u, vn, gb, ma, mb, k_blocks, vt_blocks, k_means = _inproj(
            xp, norm_g[l], wqkv_t_bf, w_bf, sgu_ln_g[l], sgu_ln_b[l], PROMPT_TOKEN_TILE,
            stack_shape=(depth, b, D_ATTN, s), kv_stack=kv_stack, layer=l, seq_len=s)
        kv_stack = (k_all, v_all)
        q, k, v, ga_s, u_s, vn_s, gb_s, ma_s, mb_s = _inproj(
            xs, norm_g[l], wqkv_bf, w_bf, sgu_ln_g[l], sgu_ln_b[l], CHUNK)

        attn, attn_s = _attention(
            rel_bias, q_t, k_blocks, vt_blocks, k_means, bias_tiles,
            q.reshape(db, t_new, D_ATTN), k.reshape(db, t_new, D_ATTN), v.reshape(db, t_new, D_ATTN),
            cache_kt, cache_vt, l, page_table, sbias, SAMPLE_BLOCKS_PER_STEP)

        xp = _merge(xp, attn.reshape(b * s, D_ATTN), ga, u, vn, gb, ma, mb, mix_p, bias_p,
                    wpa_bf, wpb_bf, wo_bf, final_norm_g, MERGE_TOKEN_TILE, final)
        xs = _merge(xs, attn_s.reshape(db * t_new, D_ATTN), ga_s, u_s, vn_s, gb_s, ma_s, mb_s, mix_s, bias_s,
                    wpa_bf, wpb_bf, wo_bf, final_norm_g, CHUNK, final)
        k_s.append(k.reshape(db, t_new, N_HEADS, HEAD_DIM))
        v_s.append(v.reshape(db, t_new, N_HEADS, HEAD_DIM))
        sgu_rows.append(vn_s.reshape(db, t_new, d_sgu))

    k_prompt = jnp.transpose(k_all.reshape(depth, b, N_HEADS, HEAD_DIM, s), (0, 1, 4, 2, 3))
    v_prompt = jnp.transpose(v_all.reshape(depth, b, N_HEADS, HEAD_DIM, s), (0, 1, 4, 2, 3))
    return (xp.reshape(b, s, d_model), xs.reshape(db, t_new, d_model),
            k_prompt, v_prompt, jnp.stack(k_s), jnp.stack(v_s), jnp.stack(sgu_rows))
```

```python
import functools
import math

import numpy as np
import jax
import jax.numpy as jnp
from jax import lax
from jax.experimental import pallas as pl
from jax.experimental.pallas import tpu as pltpu

N_HEADS = 8
HEAD_DIM = 64
D_ATTN = N_HEADS * HEAD_DIM
MOBA_BLOCK = 256
MOBA_TOPK = 3
N_GROUPS = 8
CHUNK = 128
N_BUCKETS = 32
MAX_DISTANCE = 128
EPS = 1e-6
LN_EPS = 1e-5
SCALE = HEAD_DIM ** -0.5
LOG2_E = math.log2(math.e)

F32 = jnp.float32
BF16 = jnp.bfloat16
NEG_INF = float("-inf")
NT_DIMS = (((1,), (1,)), ((), ()))

VMEM_LIMIT_BYTES = 56 * 1024 * 1024


def _bucket_thresholds():
    max_exact = N_BUCKETS // 2
    d = np.arange(max_exact, 4 * MAX_DISTANCE, dtype=np.float64)
    log_b = max_exact + (np.log(d / max_exact) / math.log(MAX_DISTANCE / max_exact) * (N_BUCKETS - max_exact)).astype(np.int64)
    bucket = np.minimum(log_b, N_BUCKETS - 1)
    return [int(d[np.argmax(bucket >= b)]) for b in range(max_exact + 1, N_BUCKETS)]


BUCKET_THRESHOLDS = _bucket_thresholds()
FAR_DISTANCE = BUCKET_THRESHOLDS[-1]


def _rel_bucket(dist):
    max_exact = N_BUCKETS // 2
    log_b = jnp.full(dist.shape, max_exact, jnp.int32)
    for thr in BUCKET_THRESHOLDS:
        log_b = log_b + jnp.where(dist >= thr, 1, 0)
    return jnp.where(dist < max_exact, dist, log_b)


def _gelu_tanh(x):
    return 0.5 * x * (1.0 + jnp.tanh(math.sqrt(2.0 / math.pi) * (x + 0.044715 * (x * x * x))))


def _silu(x):
    return x * jax.nn.sigmoid(x)


def _prompt_bias_kernel(tab_ref, o_ref):
    h = pl.program_id(0)
    c = pl.program_id(1)
    shape = o_ref.shape
    key = lax.broadcasted_iota(jnp.int32, shape, 0)
    qry = lax.broadcasted_iota(jnp.int32, shape, 1)
    dist = c * MOBA_BLOCK + qry - key
    bucket = _rel_bucket(dist)
    val = jnp.full(shape, tab_ref[N_BUCKETS - 1, h], F32)
    for b in range(N_BUCKETS - 1):
        val = jnp.where(bucket == b, tab_ref[b, h], val)
    o_ref[...] = jnp.where(dist < 0, NEG_INF, val)


def _prompt_bias_tiles(rel_bias):
    assert MOBA_BLOCK + 1 >= FAR_DISTANCE
    return pl.pallas_call(
        _prompt_bias_kernel,
        grid=(N_HEADS, 2),
        in_specs=[pl.BlockSpec(memory_space=pltpu.SMEM)],
        out_specs=pl.BlockSpec((None, None, MOBA_BLOCK, MOBA_BLOCK), lambda h, c: (h, c, 0, 0)),
        out_shape=jax.ShapeDtypeStruct((N_HEADS, 2, MOBA_BLOCK, MOBA_BLOCK), F32),
        compiler_params=pltpu.CompilerParams(dimension_semantics=("arbitrary", "arbitrary")),
        name="prompt_bias_tiles",
    )(rel_bias)


def _sample_bias_kernel(tab_ref, o_ref, *, t_new):
    shape = o_ref.shape
    row = lax.broadcasted_iota(jnp.int32, shape, 0)
    lane = lax.broadcasted_iota(jnp.int32, shape, 1)
    t = row % t_new
    dist = jnp.where(lane < MOBA_BLOCK, MOBA_BLOCK + t - lane,
                     jnp.where(lane < MOBA_BLOCK + 128, t - (lane - MOBA_BLOCK), FAR_DISTANCE))
    bucket = _rel_bucket(dist)
    tab = tab_ref[...]
    val = jnp.broadcast_to(tab[:, N_BUCKETS - 1:N_BUCKETS], shape)
    for b in range(N_BUCKETS - 1):
        val = jnp.where(bucket == b, tab[:, b:b + 1], val)
    o_ref[...] = jnp.where(dist < 0, NEG_INF, val)


def _sample_bias(rel_bias, t_new):
    rows = N_HEADS * t_new
    tab_rows = jnp.repeat(rel_bias.T, t_new, axis=0)
    return pl.pallas_call(
        functools.partial(_sample_bias_kernel, t_new=t_new),
        out_shape=jax.ShapeDtypeStruct((rows, MOBA_BLOCK + 256), F32),
        name="sample_bias",
    )(tab_rows)


def _inproj_kernel(*refs, transposed_qkv, n_aliased):
    x_ref, g_ref, wqkv_ref, w_ref, lng_ref, lnb_ref = refs[:6]
    q_ref, k_ref, v_ref, ga_ref, u_ref, vn_ref, gb_ref, ma_ref, mb_ref = refs[6 + n_aliased:6 + n_aliased + 9]
    x = x_ref[...]
    d_model = x.shape[-1]
    d_sgu = u_ref.shape[-1]
    h = (x * lax.rsqrt(jnp.mean(x * x, axis=-1, keepdims=True) + EPS) * g_ref[...]).astype(BF16)

    if transposed_qkv:
        qkv_t = lax.dot_general(wqkv_ref[...], h, NT_DIMS, preferred_element_type=F32)
        q_ref[...] = qkv_t[0:D_ATTN].astype(q_ref.dtype)
        for ref, rows in ((k_ref, qkv_t[D_ATTN:2 * D_ATTN]), (v_ref, qkv_t[2 * D_ATTN:3 * D_ATTN])):
            if len(ref.shape) == 2:
                ref[...] = rows
            else:
                ref[0] = rows
                for later in range(1, ref.shape[0]):
                    ref[later] = jnp.zeros_like(rows)
        kb_ref, vb_ref, km_ref = refs[6 + n_aliased + 9:]
        k_rows = qkv_t[D_ATTN:2 * D_ATTN].T
        kb_ref[...] = k_rows.astype(BF16)
        vb_ref[...] = qkv_t[2 * D_ATTN:3 * D_ATTN].astype(BF16)
        km_ref[...] = jnp.sum(k_rows, axis=0, keepdims=True) * (1.0 / k_rows.shape[0])
    else:
        qkv = jnp.dot(h, wqkv_ref[...], preferred_element_type=F32)
        q_ref[...] = qkv[:, 0:D_ATTN]
        k_ref[...] = qkv[:, D_ATTN:2 * D_ATTN]
        v_ref[...] = qkv[:, 2 * D_ATTN:3 * D_ATTN]

    def seg(lo, width):
        return jnp.dot(h, w_ref[:, lo:lo + width], preferred_element_type=F32)

    act = ga_ref.dtype
    ga_ref[...] = seg(0, D_ATTN).astype(act)
    base = D_ATTN
    u_ref[...] = _gelu_tanh(seg(base, d_sgu)).astype(act)
    vs = _gelu_tanh(seg(base + d_sgu, d_sgu))
    mu = jnp.mean(vs, axis=-1, keepdims=True)
    var = jnp.mean(jnp.square(vs - mu), axis=-1, keepdims=True)
    vn_ref[...] = ((vs - mu) * lax.rsqrt(var + LN_EPS) * lng_ref[...] + lnb_ref[...]).astype(act)
    gb_ref[...] = seg(base + 2 * d_sgu, d_sgu).astype(act)
    ma_ref[...] = seg(base + 3 * d_sgu, d_model).astype(act)
    mb_ref[...] = seg(base + 3 * d_sgu + d_model, d_model).astype(act)


def _inproj(x, norm_g, wqkv_bf, w_bf, ln_g, ln_b, tm, stack_shape=None, kv_stack=None, layer=None, seq_len=None):
    n_tok, d_model = x.shape
    d_sgu = ln_g.shape[-1]
    assert n_tok % tm == 0
    widths = [D_ATTN] + [d_sgu] * 3 + [d_model] * 2
    assert sum(widths) == w_bf.shape[1]
    row = lambda i: (i, 0)
    fixed = lambda i: (0, 0)
    transposed = stack_shape is not None
    in_specs = [pl.BlockSpec((tm, d_model), row),
                pl.BlockSpec((1, d_model), fixed),
                pl.BlockSpec(wqkv_bf.shape, fixed),
                pl.BlockSpec(w_bf.shape, fixed),
                pl.BlockSpec((1, d_sgu), fixed),
                pl.BlockSpec((1, d_sgu), fixed)]
    args = [x, norm_g.reshape(1, d_model), wqkv_bf, w_bf, ln_g.reshape(1, d_sgu), ln_b.reshape(1, d_sgu)]
    rest_specs = [pl.BlockSpec((tm, w), row) for w in widths]
    act_dtype = BF16 if transposed else F32
    rest_shapes = [jax.ShapeDtypeStruct((n_tok, w), act_dtype) for w in widths]
    aliases = {}
    if transposed:
        assert seq_len % tm == 0 and n_tok % seq_len == 0
        tiles = seq_len // tm
        batch = n_tok // seq_len
        if kv_stack is not None:
            in_specs += [pl.BlockSpec(memory_space=pl.ANY)] * 2
            args += list(kv_stack)
            aliases = {6: 1, 7: 2}
            stack_spec = pl.BlockSpec((None, None, D_ATTN, tm), lambda i: (layer, i // tiles, 0, i % tiles))
        else:
            assert layer == 0
            stack_spec = pl.BlockSpec((stack_shape[0], None, D_ATTN, tm), lambda i: (0, i // tiles, 0, i % tiles))
        qkv_specs = [pl.BlockSpec((None, D_ATTN, tm), lambda i: (i // tiles, 0, i % tiles)), stack_spec, stack_spec]
        qkv_shapes = [jax.ShapeDtypeStruct((batch, D_ATTN, seq_len), BF16),
                      jax.ShapeDtypeStruct(stack_shape, F32), jax.ShapeDtypeStruct(stack_shape, F32)]
    else:
        qkv_specs = [pl.BlockSpec((tm, D_ATTN), row)] * 3
        qkv_shapes = [jax.ShapeDtypeStruct((n_tok, D_ATTN), F32)] * 3
    block_specs, block_shapes = [], []
    if transposed:
        assert tm == MOBA_BLOCK
        blk_map = lambda i: (i // tiles, i % tiles, 0, 0)
        block_specs = [pl.BlockSpec((None, None, tm, D_ATTN), blk_map),
                       pl.BlockSpec((None, None, D_ATTN, tm), blk_map),
                       pl.BlockSpec((None, None, 1, D_ATTN), blk_map)]
        block_shapes = [jax.ShapeDtypeStruct((batch, tiles, tm, D_ATTN), BF16),
                        jax.ShapeDtypeStruct((batch, tiles, D_ATTN, tm), BF16),
                        jax.ShapeDtypeStruct((batch, tiles, 1, D_ATTN), F32)]
    return pl.pallas_call(
        functools.partial(_inproj_kernel, transposed_qkv=transposed, n_aliased=len(aliases)),
        grid=(n_tok // tm,),
        in_specs=in_specs,
        out_specs=qkv_specs + rest_specs + block_specs,
        out_shape=qkv_shapes + rest_shapes + block_shapes,
        input_output_aliases=aliases,
        compiler_params=pltpu.CompilerParams(dimension_semantics=("arbitrary",),
                                             vmem_limit_bytes=VMEM_LIMIT_BYTES),
        name="inproj",
    )(*args)


def _prompt_attention(qt, sub, tab_ref, qt_ref, kb_scr, vb_scr, km_ref, bias_ref, o_ref,
                      kmbd_scr, neg_scr, qh_scr, s_scr, m_scr, l_scr, acc_scr):
    n_blk = kb_scr.shape[0]
    tq = qt_ref.shape[1]
    scoring = sub == 0
    weighting = sub == 1

    @pl.when(scoring & (qt == 0))
    def _():
        lane_head = lax.broadcasted_iota(jnp.int32, (1, D_ATTN), 1) // HEAD_DIM
        for n in range(n_blk):
            k_mean = km_ref[n]
            for h in range(N_HEADS):
                kmbd_scr[h * n_blk + n:h * n_blk + n + 1, :] = jnp.where(lane_head == h, k_mean, 0.0)

    def scaled_q():
        return (qt_ref[...].astype(F32) * SCALE).astype(BF16)

    def select_blocks():
        gate = jnp.dot(kmbd_scr[...].astype(BF16), scaled_q(), preferred_element_type=F32)
        blk = lax.broadcasted_iota(jnp.int32, (n_blk, tq), 0)
        for h in range(N_HEADS):
            g = gate[h * n_blk:(h + 1) * n_blk, :]
            beaten = jnp.zeros((n_blk, tq), jnp.int32)
            for m in range(n_blk):
                gm = g[m:m + 1, :]
                wins = (gm > g) | ((gm == g) & (m < blk))
                beaten = beaten + jnp.where(wins, 1, 0) * (m < qt).astype(jnp.int32)
            selected = (blk < qt) & (beaten < MOBA_TOPK)
            far = jnp.where(blk < qt - 1, tab_ref[N_BUCKETS - 1, h], 0.0)
            neg_scr[h] = jnp.where(selected, far, NEG_INF)

    @pl.when(scoring)
    def _():
        q_t = scaled_q()
        half = lax.broadcasted_iota(jnp.int32, (2 * HEAD_DIM, tq), 0) // HEAD_DIM
        for h in range(N_HEADS):
            pair = h // 2
            q_pair = q_t[pair * 2 * HEAD_DIM:(pair + 1) * 2 * HEAD_DIM, :]
            qh_scr[h] = jnp.where(half == h % 2, q_pair, jnp.zeros_like(q_pair))
            m_scr[h] = jnp.full(m_scr.shape[1:], NEG_INF, F32)

    def scores(h, n):
        pair = h // 2
        k_pair = kb_scr[n, :, pair * 2 * HEAD_DIM:(pair + 1) * 2 * HEAD_DIM]
        return jnp.dot(k_pair, qh_scr[h], preferred_element_type=F32)

    def sublane_groups(a):
        return a.reshape(a.shape[0] // 8, 8, a.shape[1])

    n_group = s_scr.shape[0]
    for g in range(N_HEADS // n_group):
        heads = range(g * n_group, (g + 1) * n_group)

        def extra_bias(h, n, kind):
            if kind == "own":
                return bias_ref[h, 0]
            row = neg_scr[h, pl.ds(n, 1), :]
            return bias_ref[h, 1] + row if kind == "prev" else row

        def score_blocks(blocks):
            for h in heads:
                part = m_scr[h]
                for n, kind in blocks:
                    s = (scores(h, n) + extra_bias(h, n, kind)) * LOG2_E
                    s_scr[h % n_group, n] = s
                    part = jnp.maximum(part, jnp.max(sublane_groups(s), axis=0))
                m_scr[h] = part

        n_far = qt - 1

        @pl.when(scoring & (qt == 0))
        def _():
            score_blocks([(qt, "own")])

        @pl.when(scoring & (qt >= 1) & (n_far % 2 == 0))
        def _(g=g):
            score_blocks([(qt, "own")])
            if g == 0:
                select_blocks()
            score_blocks([(qt - 1, "prev")])

        @pl.when(scoring & (qt >= 1) & (n_far % 2 == 1))
        def _(g=g):
            score_blocks([(qt, "own")])
            if g == 0:
                select_blocks()
            score_blocks([(qt - 1, "prev"), (qt - 2, "far")])

        def far_body(i, carry):
            score_blocks([(2 * i, "far"), (2 * i + 1, "far")])
            return carry

        lax.fori_loop(0, jnp.where(scoring, jnp.maximum(n_far, 0) // 2, 0), far_body, 0)

        m_fin = {h: jnp.max(m_scr[h], axis=0, keepdims=True) for h in heads}

        @pl.when(weighting)
        def _():
            for h in heads:
                l_scr[h] = jnp.zeros(l_scr.shape[1:], F32)
                acc_scr[h * HEAD_DIM:(h + 1) * HEAD_DIM, :] = jnp.zeros((HEAD_DIM, tq), F32)

        def weight_blocks(blocks):
            for h in heads:
                rows = slice(h * HEAD_DIM, (h + 1) * HEAD_DIM)
                l_part = l_scr[h]
                acc = acc_scr[rows, :]
                for n in blocks:
                    p = jnp.exp2(s_scr[h % n_group, n] - m_fin[h])
                    l_part = l_part + jnp.sum(sublane_groups(p), axis=0)
                    v_t = vb_scr[n, rows, :]
                    acc = acc + jnp.dot(v_t, p.astype(BF16), preferred_element_type=F32)
                l_scr[h] = l_part
                acc_scr[rows, :] = acc

        n_all = qt + 1

        @pl.when(weighting & (n_all % 2 == 1))
        def _():
            weight_blocks([qt])

        def weight_body(i, carry):
            weight_blocks([2 * i, 2 * i + 1])
            return carry

        lax.fori_loop(0, jnp.where(weighting, n_all // 2, 0), weight_body, 0)

        @pl.when(weighting)
        def _():
            for h in heads:
                rows = slice(h * HEAD_DIM, (h + 1) * HEAD_DIM)
                acc_scr[rows, :] = acc_scr[rows, :] / jnp.sum(l_scr[h], axis=0, keepdims=True)

    @pl.when(weighting)
    def _():
        o_ref[...] = acc_scr[...].T.astype(o_ref.dtype)


def _sample_attention(step, n_steps, q_ref, kn_ref, vn_ref, sb_ref, k_pages, v_pages, o_ref,
                      g_scr, m_scr, l_scr, o_scr, *, pages_per_block, n_blk):
    blocks_per_step = len(k_pages) // pages_per_block
    t_new = kn_ref.shape[0]
    rows = q_ref.shape[0]
    page = k_pages[0].shape[-1]

    row_head = lax.broadcasted_iota(jnp.int32, (rows, D_ATTN), 0) // t_new
    lane_head = lax.broadcasted_iota(jnp.int32, (rows, D_ATTN), 1) // HEAD_DIM
    own_lanes = row_head == lane_head
    q_rows = jnp.where(own_lanes, q_ref[...] * SCALE, 0.0)
    q_bf = q_rows.astype(BF16)
    lane = lax.broadcasted_iota(jnp.int32, (rows, 128), 1)

    @pl.when(step == 0)
    def _():
        g_scr[...] = jnp.full(g_scr.shape, NEG_INF, F32)
        m_scr[...] = jnp.full(m_scr.shape, NEG_INF, F32)
        l_scr[...] = jnp.zeros(l_scr.shape, F32)

    def block_t(pages, jb):
        parts = [pages[jb * pages_per_block + jp][...].reshape(D_ATTN, page) for jp in range(pages_per_block)]
        return jnp.concatenate(parts, axis=1).astype(BF16)

    def fold_pages(a):
        return sum(a[:, jp * page:(jp + 1) * page] for jp in range(pages_per_block))

    far_bias = sb_ref[:, MOBA_BLOCK + 128:MOBA_BLOCK + 129]
    g_all, m_all, l_all = g_scr[...], m_scr[...], l_scr[...]
    blocks = [step * blocks_per_step + jb for jb in range(blocks_per_step)]
    scores = [jnp.dot(q_bf, block_t(k_pages, jb), preferred_element_type=F32)
              for jb in range(blocks_per_step)]
    probs = []
    for n, s in zip(blocks, scores):
        g_all = jnp.where(lane == n, jnp.sum(fold_pages(s), axis=1, keepdims=True), g_all)
        newest = (jnp.zeros(s.shape, jnp.int32) + n) == n_blk - 1
        s = s + jnp.where(newest, sb_ref[:, 0:MOBA_BLOCK], far_bias)
        m_blk = jnp.max(s, axis=1, keepdims=True)
        p = jnp.exp(s - m_blk)
        l_all = jnp.where(lane == n, jnp.sum(fold_pages(p), axis=1, keepdims=True), l_all)
        m_all = jnp.where(lane == n, m_blk, m_all)
        probs.append(p.astype(BF16))
    g_scr[...], m_scr[...], l_scr[...] = g_all, m_all, l_all
    for jb, (n, p) in enumerate(zip(blocks, probs)):
        o_scr[n] = lax.dot_general(p, block_t(v_pages, jb), NT_DIMS, preferred_element_type=F32)

    @pl.when(step == n_steps - 1)
    def _():
        g = g_all
        beaten = jnp.zeros(g.shape, jnp.int32)
        for m in range(n_blk):
            gm = g[:, m:m + 1]
            beaten = beaten + jnp.where((gm > g) | ((gm == g) & (m < lane)), 1, 0)
        selected = (beaten < MOBA_TOPK) & (lane < n_blk)
        m_sel = jnp.where(selected, m_scr[...], NEG_INF)

        s_loc = jnp.full((rows, 128), NEG_INF, F32)
        for t in range(t_new):
            col = jnp.sum(q_rows * kn_ref[t:t + 1, :], axis=1, keepdims=True)
            s_loc = jnp.where(lane == t, col, s_loc)
        s_loc = s_loc + sb_ref[:, MOBA_BLOCK:MOBA_BLOCK + 128]

        m_tot = jnp.maximum(jnp.max(m_sel, axis=1, keepdims=True), jnp.max(s_loc, axis=1, keepdims=True))
        w_blk = jnp.where(selected, jnp.exp(m_sel - m_tot), 0.0)
        p_loc = jnp.exp(s_loc - m_tot)
        denom = jnp.sum(w_blk * l_all, axis=1, keepdims=True) + jnp.sum(p_loc, axis=1, keepdims=True)
        acc = jnp.zeros((rows, D_ATTN), F32)
        for n in range(n_blk):
            acc = acc + w_blk[:, n:n + 1] * o_scr[n]
        for t in range(t_new):
            acc = acc + p_loc[:, t:t + 1] * vn_ref[t:t + 1, :]
        acc = jnp.where(own_lanes, acc / denom, 0.0)
        folded = acc
        for h in range(1, N_HEADS):
            folded = folded + pltpu.roll(acc, rows - h * t_new, 0)
        o_ref[...] = folded[:t_new, :]


def _attention_kernel(pt_ref, tab_ref, qt_ref, kb_ref, vb_ref, km_ref, bias_ref, sq_ref, skn_ref, svn_ref, sb_ref,
                      ck_hbm, cv_hbm, o_prompt, o_sample, *scratch,
                      layer, n_windows, pages_per_block, n_blk_sample, steps_per_seq):
    prompt_scratch, sample_scratch = scratch[:7], scratch[7:11]
    kbuf, vbuf, sem = scratch[11:]
    qt = pl.program_id(1)
    sub = pl.program_id(2)
    linear = (pl.program_id(0) * pl.num_programs(1) + qt) * pl.num_programs(2) + sub
    n_total = pl.num_programs(0) * pl.num_programs(1) * pl.num_programs(2)

    def page_copy(hbm, buf, which, slot, w, page_id):
        return pltpu.make_async_copy(hbm.at[layer, page_id], buf.at[slot, w], sem.at[which, slot])

    def start_pages(lin, slot):
        seq = lin // steps_per_seq
        first = (lin % steps_per_seq) * n_windows
        for w in range(n_windows):
            page_id = pt_ref[seq, first + w]
            page_copy(ck_hbm, kbuf, 0, slot, w, page_id).start()
            page_copy(cv_hbm, vbuf, 1, slot, w, page_id).start()

    @pl.when(linear == 0)
    def _():
        start_pages(linear, 0)

    @pl.when(linear + 1 < n_total)
    def _():
        start_pages(linear + 1, (linear + 1) % 2)

    slot = linear % 2
    for w in range(n_windows):
        page_copy(ck_hbm, kbuf, 0, slot, w, 0).wait()
        page_copy(cv_hbm, vbuf, 1, slot, w, 0).wait()
    k_pages = [kbuf.at[slot, w] for w in range(n_windows)]
    v_pages = [vbuf.at[slot, w] for w in range(n_windows)]

    _sample_attention(linear % steps_per_seq, steps_per_seq, sq_ref, skn_ref, svn_ref, sb_ref, k_pages, v_pages,
                      o_sample, *sample_scratch, pages_per_block=pages_per_block, n_blk=n_blk_sample)
    _prompt_attention(qt, sub, tab_ref, qt_ref, kb_ref, vb_ref, km_ref, bias_ref, o_prompt, *prompt_scratch)


def _attention(rel_bias, q_t, k_blocks, vt_blocks, k_means, bias_tiles,
               q, k_new, v_new, cache_kt, cache_vt, layer, page_table, sbias, blocks_per_step):
    b, _, s = q_t.shape
    assert s % MOBA_BLOCK == 0
    n_blk = s // MOBA_BLOCK
    tq = MOBA_BLOCK
    n_qt = s // tq
    n_pass = 2

    db, t_new, _ = q.shape
    page = cache_kt.shape[-1]
    n_pages = page_table.shape[1]
    pages_per_block = MOBA_BLOCK // page
    assert page == 128 and MOBA_BLOCK % page == 0
    assert (n_pages * page) % MOBA_BLOCK == 0 and t_new <= 128
    n_blk_s = n_pages // pages_per_block
    assert MOBA_TOPK <= n_blk_s <= 128 and n_blk_s % blocks_per_step == 0
    assert MOBA_BLOCK + 1 >= FAR_DISTANCE
    steps_per_seq = n_blk_s // blocks_per_step
    assert b * n_qt * n_pass == db * steps_per_seq
    rows = N_HEADS * t_new
    q_rows = jnp.tile(q, (1, N_HEADS, 1))
    pages_per_step = blocks_per_step * pages_per_block

    def linear(i, j, k):
        return (i * n_qt + j) * n_pass + k

    per_batch = lambda i, j, k, pt: (i, 0, 0, 0)
    per_seq = lambda i, j, k, pt: (linear(i, j, k) // steps_per_seq, 0, 0)
    tok_spec = pl.BlockSpec((None, t_new, D_ATTN), per_seq)
    grid_spec = pltpu.PrefetchScalarGridSpec(
        num_scalar_prefetch=1,
        grid=(b, n_qt, n_pass),
        in_specs=[pl.BlockSpec(memory_space=pltpu.SMEM),
                  pl.BlockSpec((None, D_ATTN, tq), lambda i, j, k, pt: (i, 0, j)),
                  pl.BlockSpec((None, n_blk, MOBA_BLOCK, D_ATTN), per_batch),
                  pl.BlockSpec((None, n_blk, D_ATTN, MOBA_BLOCK), per_batch),
                  pl.BlockSpec((None, n_blk, 1, D_ATTN), per_batch),
                  pl.BlockSpec(bias_tiles.shape, lambda i, j, k, pt: (0, 0, 0, 0)),
                  pl.BlockSpec((None, rows, D_ATTN), per_seq), tok_spec, tok_spec,
                  pl.BlockSpec(sbias.shape, lambda i, j, k, pt: (0, 0)),
                  pl.BlockSpec(memory_space=pl.ANY), pl.BlockSpec(memory_space=pl.ANY)],
        out_specs=[pl.BlockSpec((None, tq, D_ATTN), lambda i, j, k, pt: (i, j, 0)), tok_spec],
        scratch_shapes=[pltpu.VMEM((N_HEADS * n_blk, D_ATTN), F32),
                        pltpu.VMEM((N_HEADS, n_blk, tq), F32),
                        pltpu.VMEM((N_HEADS, 2 * HEAD_DIM, tq), BF16),
                        pltpu.VMEM((N_HEADS, n_blk, MOBA_BLOCK, tq), F32),
                        pltpu.VMEM((N_HEADS, 8, tq), F32),
                        pltpu.VMEM((N_HEADS, 8, tq), F32),
                        pltpu.VMEM((D_ATTN, tq), F32),
                        pltpu.VMEM((rows, 128), F32), pltpu.VMEM((rows, 128), F32), pltpu.VMEM((rows, 128), F32),
                        pltpu.VMEM((n_blk_s, rows, D_ATTN), F32),
                        pltpu.VMEM((2, pages_per_step, N_HEADS, HEAD_DIM, page), F32),
                        pltpu.VMEM((2, pages_per_step, N_HEADS, HEAD_DIM, page), F32),
                        pltpu.SemaphoreType.DMA((2, 2))],
    )
    return pl.pallas_call(
        functools.partial(_attention_kernel, layer=layer, n_windows=pages_per_step,
                          pages_per_block=pages_per_block, n_blk_sample=n_blk_s, steps_per_seq=steps_per_seq),
        grid_spec=grid_spec,
        out_shape=[jax.ShapeDtypeStruct((b, s, D_ATTN), BF16), jax.ShapeDtypeStruct((db, t_new, D_ATTN), F32)],
        compiler_params=pltpu.CompilerParams(dimension_semantics=("arbitrary", "arbitrary", "arbitrary"),
                                             vmem_limit_bytes=VMEM_LIMIT_BYTES),
        name="moba_attention",
    )(page_table, rel_bias, q_t, k_blocks, vt_blocks, k_means, bias_tiles, q_rows, k_new, v_new, sbias,
      cache_kt, cache_vt)


def _merge_kernel(x_ref, attn_ref, ga_ref, u_ref, vn_ref, gb_ref, ma_ref, mb_ref,
                  mix_ref, sb_ref, wpa_ref, wpb_ref, wo_ref, fg_ref, o_ref, *, final):
    tm = x_ref.shape[0]
    d_sgu = u_ref.shape[-1]
    group_dim = d_sgu // N_GROUPS
    first_of_pair = lax.broadcasted_iota(jnp.int32, (CHUNK, 2 * group_dim), 1) < group_dim
    z_chunks = []
    for c in range(tm // CHUNK):
        vn = vn_ref[c * CHUNK:(c + 1) * CHUNK, :].astype(BF16)
        z_pairs = []
        for j in range(N_GROUPS // 2):
            v_pair = vn[:, j * 2 * group_dim:(j + 1) * 2 * group_dim]
            z_a = jnp.dot(mix_ref[2 * j], v_pair, preferred_element_type=F32)
            z_b = jnp.dot(mix_ref[2 * j + 1], v_pair, preferred_element_type=F32)
            z_pairs.append(jnp.where(first_of_pair, z_a, z_b))
        z_chunks.append(jnp.concatenate(z_pairs, axis=1) + sb_ref[...])
    z = jnp.concatenate(z_chunks, axis=0) if len(z_chunks) > 1 else z_chunks[0]
    y_b = (u_ref[...].astype(F32) * z) * _silu(gb_ref[...].astype(F32))
    y_a = attn_ref[...].astype(F32) * _silu(ga_ref[...].astype(F32))
    p_a = jnp.dot(y_a.astype(BF16), wpa_ref[...], preferred_element_type=F32)
    p_b = jnp.dot(y_b.astype(BF16), wpb_ref[...], preferred_element_type=F32)
    mixed = (jax.nn.sigmoid(ma_ref[...].astype(F32)) * p_a
             + jax.nn.sigmoid(mb_ref[...].astype(F32)) * p_b)
    out = x_ref[...] + jnp.dot(mixed.astype(BF16), wo_ref[...], preferred_element_type=F32)
    if final:
        out = out * lax.rsqrt(jnp.mean(out * out, axis=-1, keepdims=True) + EPS) * fg_ref[...]
    o_ref[...] = out


def _merge(x, attn, ga, u, vn, gb, ma, mb, mix_bf, sgu_bias, wpa_bf, wpb_bf, wo_bf, final_g, tm, final):
    n_tok, d_model = x.shape
    d_sgu = u.shape[-1]
    assert n_tok % tm == 0 and tm % CHUNK == 0
    row = lambda i: (i, 0)
    fixed2 = lambda i: (0, 0)
    narrow = pl.BlockSpec((tm, d_sgu), row)
    wide = pl.BlockSpec((tm, d_model), row)
    return pl.pallas_call(
        functools.partial(_merge_kernel, final=final),
        grid=(n_tok // tm,),
        in_specs=[wide, pl.BlockSpec((tm, D_ATTN), row), pl.BlockSpec((tm, D_ATTN), row),
                  narrow, narrow, narrow, wide, wide,
                  pl.BlockSpec(mix_bf.shape, lambda i: (0, 0, 0)),
                  pl.BlockSpec(sgu_bias.shape, fixed2),
                  pl.BlockSpec(wpa_bf.shape, fixed2),
                  pl.BlockSpec(wpb_bf.shape, fixed2),
                  pl.BlockSpec(wo_bf.shape, fixed2),
                  pl.BlockSpec((1, d_model), fixed2)],
        out_specs=wide,
        out_shape=jax.ShapeDtypeStruct((n_tok, d_model), F32),
        compiler_params=pltpu.CompilerParams(dimension_semantics=("arbitrary",),
                                             vmem_limit_bytes=VMEM_LIMIT_BYTES),
        name="merge",
    )(x, attn, ga, u, vn, gb, ma, mb, mix_bf, sgu_bias, wpa_bf, wpb_bf, wo_bf, final_g.reshape(1, d_model))


PROMPT_TOKEN_TILE = 256
MERGE_TOKEN_TILE = 512
SAMPLE_BLOCKS_PER_STEP = 8


def kernel(x_prompt, x_sample, cache_k, cache_v, page_table, rel_bias, norm_g, w_in, sgu_ln_g, sgu_ln_b,
           w_s, b_s, w_pa, w_pb, w_o, final_norm_g):
    b, s, d_model = x_prompt.shape
    db, t_new, _ = x_sample.shape
    depth = w_in.shape[0]
    d_sgu = sgu_ln_g.shape[-1]
    group_dim = d_sgu // N_GROUPS
    assert s % CHUNK == 0 and (db * t_new) % CHUNK == 0 and CHUNK % t_new == 0
    assert cache_k.shape[3] == N_HEADS and cache_k.shape[4] == HEAD_DIM

    bias_tiles = _prompt_bias_tiles(rel_bias)
    sbias = _sample_bias(rel_bias, t_new)
    cache_kt = jnp.transpose(cache_k, (0, 1, 3, 4, 2))
    cache_vt = jnp.transpose(cache_v, (0, 1, 3, 4, 2))

    xp = x_prompt.reshape(b * s, d_model)
    xs = x_sample.reshape(db * t_new, d_model)
    kv_stack = None
    k_s, v_s, sgu_rows = [], [], []
    for l in range(depth):
        final = l == depth - 1
        wqkv = w_in[l][:, :3 * D_ATTN]
        wqkv_bf, wqkv_t_bf = wqkv.astype(BF16), wqkv.T.astype(BF16)
        w_bf = w_in[l][:, 3 * D_ATTN:].astype(BF16)
        wpa_bf, wpb_bf, wo_bf = w_pa[l].astype(BF16), w_pb[l].astype(BF16), w_o[l].astype(BF16)
        mix_p = jnp.tril(w_s[l]).astype(BF16)
        bias_p = jnp.repeat(b_s[l].T, group_dim, axis=1)
        corner = jnp.tril(w_s[l][:, :t_new, :t_new])
        eye = jnp.eye(CHUNK // t_new, dtype=F32)
        mix_s = jnp.einsum("ab,gts->gatbs", eye, corner).reshape(N_GROUPS, CHUNK, CHUNK).astype(BF16)
        bias_s = jnp.tile(jnp.repeat(b_s[l][:, :t_new].T, group_dim, axis=1), (CHUNK // t_new, 1))

        q_t, k_all, v_all, ga, u, vn, gb, ma, mb, k_blocks, vt_blocks, k_means = _inproj(
            xp, norm_g[l], wqkv_t_bf, w_bf, sgu_ln_g[l], sgu_ln_b[l], PROMPT_TOKEN_TILE,
            stack_shape=(depth, b, D_ATTN, s), kv_stack=kv_stack, layer=l, seq_len=s)
        kv_stack = (k_all, v_all)
        q, k, v, ga_s, u_s, vn_s, gb_s, ma_s, mb_s = _inproj(
            xs, norm_g[l], wqkv_bf, w_bf, sgu_ln_g[l], sgu_ln_b[l], CHUNK)

        attn, attn_s = _attention(
            rel_bias, q_t, k_blocks, vt_blocks, k_means, bias_tiles,
            q.reshape(db, t_new, D_ATTN), k.reshape(db, t_new, D_ATTN), v.reshape(db, t_new, D_ATTN),
            cache_kt, cache_vt, l, page_table, sbias, SAMPLE_BLOCKS_PER_STEP)

        xp = _merge(xp, attn.reshape(b * s, D_ATTN), ga, u, vn, gb, ma, mb, mix_p, bias_p,
                    wpa_bf, wpb_bf, wo_bf, final_norm_g, MERGE_TOKEN_TILE, final)
        xs = _merge(xs, attn_s.reshape(db * t_new, D_ATTN), ga_s, u_s, vn_s, gb_s, ma_s, mb_s, mix_s, bias_s,
                    wpa_bf, wpb_bf, wo_bf, final_norm_g, CHUNK, final)
        k_s.append(k.reshape(db, t_new, N_HEADS, HEAD_DIM))
        v_s.append(v.reshape(db, t_new, N_HEADS, HEAD_DIM))
        sgu_rows.append(vn_s.reshape(db, t_new, d_sgu))

    k_prompt = jnp.transpose(k_all.reshape(depth, b, N_HEADS, HEAD_DIM, s), (0, 1, 4, 2, 3))
    v_prompt = jnp.transpose(v_all.reshape(depth, b, N_HEADS, HEAD_DIM, s), (0, 1, 4, 2, 3))
    return (xp.reshape(b, s, d_model), xs.reshape(db, t_new, d_model),
            k_prompt, v_prompt, jnp.stack(k_s), jnp.stack(v_s), jnp.stack(sgu_rows))
```

```python
import functools
import math

import numpy as np
import jax
import jax.numpy as jnp
from jax import lax
from jax.experimental import pallas as pl
from jax.experimental.pallas import tpu as pltpu

N_HEADS = 8
HEAD_DIM = 64
D_ATTN = N_HEADS * HEAD_DIM
MOBA_BLOCK = 256
MOBA_TOPK = 3
N_GROUPS = 8
CHUNK = 128
N_BUCKETS = 32
MAX_DISTANCE = 128
EPS = 1e-6
LN_EPS = 1e-5
SCALE = HEAD_DIM ** -0.5
LOG2_E = math.log2(math.e)

F32 = jnp.float32
BF16 = jnp.bfloat16
NEG_INF = float("-inf")
NT_DIMS = (((1,), (1,)), ((), ()))

LANES = 128
SUBLANES = 8
VMEM_LIMIT_BYTES = 56 * 1024 * 1024


def _bucket_thresholds():
    max_exact = N_BUCKETS // 2
    d = np.arange(max_exact, 4 * MAX_DISTANCE, dtype=np.float64)
    log_b = max_exact + (np.log(d / max_exact) / math.log(MAX_DISTANCE / max_exact) * (N_BUCKETS - max_exact)).astype(np.int64)
    bucket = np.minimum(log_b, N_BUCKETS - 1)
    return [int(d[np.argmax(bucket >= b)]) for b in range(max_exact + 1, N_BUCKETS)]


BUCKET_THRESHOLDS = _bucket_thresholds()
FAR_DISTANCE = BUCKET_THRESHOLDS[-1]


def _rel_bucket(dist):
    max_exact = N_BUCKETS // 2
    log_b = jnp.full(dist.shape, max_exact, jnp.int32)
    for thr in BUCKET_THRESHOLDS:
        log_b = log_b + jnp.where(dist >= thr, 1, 0)
    return jnp.where(dist < max_exact, dist, log_b)


def _gelu_tanh(x):
    return 0.5 * x * (1.0 + jnp.tanh(math.sqrt(2.0 / math.pi) * (x + 0.044715 * (x * x * x))))


def _silu(x):
    return x * jax.nn.sigmoid(x)


def _prompt_bias_kernel(tab_ref, o_ref):
    h = pl.program_id(0)
    c = pl.program_id(1)
    shape = o_ref.shape
    key = lax.broadcasted_iota(jnp.int32, shape, 0)
    qry = lax.broadcasted_iota(jnp.int32, shape, 1)
    dist = c * MOBA_BLOCK + qry - key
    bucket = _rel_bucket(dist)
    val = jnp.full(shape, tab_ref[N_BUCKETS - 1, h], F32)
    for b in range(N_BUCKETS - 1):
        val = jnp.where(bucket == b, tab_ref[b, h], val)
    o_ref[...] = jnp.where(dist < 0, NEG_INF, val)


def _prompt_bias_tiles(rel_bias):
    assert MOBA_BLOCK + 1 >= FAR_DISTANCE
    return pl.pallas_call(
        _prompt_bias_kernel,
        grid=(N_HEADS, 2),
        in_specs=[pl.BlockSpec(memory_space=pltpu.SMEM)],
        out_specs=pl.BlockSpec((None, None, MOBA_BLOCK, MOBA_BLOCK), lambda h, c: (h, c, 0, 0)),
        out_shape=jax.ShapeDtypeStruct((N_HEADS, 2, MOBA_BLOCK, MOBA_BLOCK), F32),
        compiler_params=pltpu.CompilerParams(dimension_semantics=("arbitrary", "arbitrary")),
        name="prompt_bias_tiles",
    )(rel_bias)


def _sample_bias_kernel(tab_ref, o_ref, *, t_new):
    shape = o_ref.shape
    row = lax.broadcasted_iota(jnp.int32, shape, 0)
    lane = lax.broadcasted_iota(jnp.int32, shape, 1)
    t = row % t_new
    dist = jnp.where(lane < MOBA_BLOCK, MOBA_BLOCK + t - lane,
                     jnp.where(lane < MOBA_BLOCK + LANES, t - (lane - MOBA_BLOCK), FAR_DISTANCE))
    bucket = _rel_bucket(dist)
    tab = tab_ref[...]
    val = jnp.broadcast_to(tab[:, N_BUCKETS - 1:N_BUCKETS], shape)
    for b in range(N_BUCKETS - 1):
        val = jnp.where(bucket == b, tab[:, b:b + 1], val)
    o_ref[...] = jnp.where(dist < 0, NEG_INF, val)


def _sample_bias(rel_bias, t_new):
    rows = N_HEADS * t_new
    tab_rows = jnp.repeat(rel_bias.T, t_new, axis=0)
    return pl.pallas_call(
        functools.partial(_sample_bias_kernel, t_new=t_new),
        out_shape=jax.ShapeDtypeStruct((rows, MOBA_BLOCK + 2 * LANES), F32),
        name="sample_bias",
    )(tab_rows)


def _inproj_kernel(*refs, transposed_qkv, n_aliased):
    x_ref, g_ref, wqkv_ref, w_ref, lng_ref, lnb_ref = refs[:6]
    q_ref, k_ref, v_ref, ga_ref, u_ref, vn_ref, gb_ref, ma_ref, mb_ref = refs[6 + n_aliased:6 + n_aliased + 9]
    x = x_ref[...]
    d_model = x.shape[-1]
    d_sgu = u_ref.shape[-1]
    h = (x * lax.rsqrt(jnp.mean(x * x, axis=-1, keepdims=True) + EPS) * g_ref[...]).astype(BF16)

    if transposed_qkv:
        qkv_t = lax.dot_general(wqkv_ref[...], h, NT_DIMS, preferred_element_type=F32)
        q_ref[...] = qkv_t[0:D_ATTN].astype(q_ref.dtype)
        for ref, rows in ((k_ref, qkv_t[D_ATTN:2 * D_ATTN]), (v_ref, qkv_t[2 * D_ATTN:3 * D_ATTN])):
            if len(ref.shape) == 2:
                ref[...] = rows
            else:
                ref[0] = rows
                for later in range(1, ref.shape[0]):
                    ref[later] = jnp.zeros_like(rows)
        kb_ref, vb_ref, km_ref = refs[6 + n_aliased + 9:]
        k_rows = qkv_t[D_ATTN:2 * D_ATTN].T
        kb_ref[...] = k_rows.astype(BF16)
        vb_ref[...] = qkv_t[2 * D_ATTN:3 * D_ATTN].astype(BF16)
        km_ref[...] = jnp.sum(k_rows, axis=0, keepdims=True) * (1.0 / k_rows.shape[0])
    else:
        qkv = jnp.dot(h, wqkv_ref[...], preferred_element_type=F32)
        q_ref[...] = qkv[:, 0:D_ATTN]
        k_ref[...] = qkv[:, D_ATTN:2 * D_ATTN]
        v_ref[...] = qkv[:, 2 * D_ATTN:3 * D_ATTN]

    def seg(lo, width):
        return jnp.dot(h, w_ref[:, lo:lo + width], preferred_element_type=F32)

    act = ga_ref.dtype
    ga_ref[...] = seg(0, D_ATTN).astype(act)
    base = D_ATTN
    u_ref[...] = _gelu_tanh(seg(base, d_sgu)).astype(act)
    vs = _gelu_tanh(seg(base + d_sgu, d_sgu))
    mu = jnp.mean(vs, axis=-1, keepdims=True)
    var = jnp.mean(jnp.square(vs - mu), axis=-1, keepdims=True)
    vn_ref[...] = ((vs - mu) * lax.rsqrt(var + LN_EPS) * lng_ref[...] + lnb_ref[...]).astype(act)
    gb_ref[...] = seg(base + 2 * d_sgu, d_sgu).astype(act)
    ma_ref[...] = seg(base + 3 * d_sgu, d_model).astype(act)
    mb_ref[...] = seg(base + 3 * d_sgu + d_model, d_model).astype(act)


def _inproj(x, norm_g, wqkv_bf, w_bf, ln_g, ln_b, tm, stack_shape=None, kv_stack=None, layer=None, seq_len=None):
    n_tok, d_model = x.shape
    d_sgu = ln_g.shape[-1]
    assert n_tok % tm == 0
    widths = [D_ATTN] + [d_sgu] * 3 + [d_model] * 2
    assert sum(widths) == w_bf.shape[1]
    row = lambda i: (i, 0)
    fixed = lambda i: (0, 0)
    transposed = stack_shape is not None
    in_specs = [pl.BlockSpec((tm, d_model), row),
                pl.BlockSpec((1, d_model), fixed),
                pl.BlockSpec(wqkv_bf.shape, fixed),
                pl.BlockSpec(w_bf.shape, fixed),
                pl.BlockSpec((1, d_sgu), fixed),
                pl.BlockSpec((1, d_sgu), fixed)]
    args = [x, norm_g.reshape(1, d_model), wqkv_bf, w_bf, ln_g.reshape(1, d_sgu), ln_b.reshape(1, d_sgu)]
    rest_specs = [pl.BlockSpec((tm, w), row) for w in widths]
    act_dtype = BF16 if transposed else F32
    rest_shapes = [jax.ShapeDtypeStruct((n_tok, w), act_dtype) for w in widths]
    aliases = {}
    if transposed:
        assert seq_len % tm == 0 and n_tok % seq_len == 0
        tiles = seq_len // tm
        batch = n_tok // seq_len
        if kv_stack is not None:
            in_specs += [pl.BlockSpec(memory_space=pl.ANY)] * 2
            args += list(kv_stack)
            aliases = {6: 1, 7: 2}
            stack_spec = pl.BlockSpec((None, None, D_ATTN, tm), lambda i: (layer, i // tiles, 0, i % tiles))
        else:
            assert layer == 0
            stack_spec = pl.BlockSpec((stack_shape[0], None, D_ATTN, tm), lambda i: (0, i // tiles, 0, i % tiles))
        qkv_specs = [pl.BlockSpec((None, D_ATTN, tm), lambda i: (i // tiles, 0, i % tiles)), stack_spec, stack_spec]
        qkv_shapes = [jax.ShapeDtypeStruct((batch, D_ATTN, seq_len), BF16),
                      jax.ShapeDtypeStruct(stack_shape, F32), jax.ShapeDtypeStruct(stack_shape, F32)]
    else:
        qkv_specs = [pl.BlockSpec((tm, D_ATTN), row)] * 3
        qkv_shapes = [jax.ShapeDtypeStruct((n_tok, D_ATTN), F32)] * 3
    block_specs, block_shapes = [], []
    if transposed:
        assert tm == MOBA_BLOCK
        blk_map = lambda i: (i // tiles, i % tiles, 0, 0)
        block_specs = [pl.BlockSpec((None, None, tm, D_ATTN), blk_map),
                       pl.BlockSpec((None, None, D_ATTN, tm), blk_map),
                       pl.BlockSpec((None, None, 1, D_ATTN), blk_map)]
        block_shapes = [jax.ShapeDtypeStruct((batch, tiles, tm, D_ATTN), BF16),
                        jax.ShapeDtypeStruct((batch, tiles, D_ATTN, tm), BF16),
                        jax.ShapeDtypeStruct((batch, tiles, 1, D_ATTN), F32)]
    return pl.pallas_call(
        functools.partial(_inproj_kernel, transposed_qkv=transposed, n_aliased=len(aliases)),
        grid=(n_tok // tm,),
        in_specs=in_specs,
        out_specs=qkv_specs + rest_specs + block_specs,
        out_shape=qkv_shapes + rest_shapes + block_shapes,
        input_output_aliases=aliases,
        compiler_params=pltpu.CompilerParams(dimension_semantics=("arbitrary",),
                                             vmem_limit_bytes=VMEM_LIMIT_BYTES),
        name="inproj",
    )(*args)


def _prompt_attention(qt, sub, tab_ref, qt_ref, kb_scr, vb_scr, km_ref, bias_ref, o_ref,
                      kmbd_scr, neg_scr, qh_scr, s_scr, m_scr, l_scr, acc_scr):
    n_blk = kb_scr.shape[0]
    tq = qt_ref.shape[1]
    scoring = sub == 0
    weighting = sub == 1

    @pl.when(scoring & (qt == 0))
    def _():
        lane_head = lax.broadcasted_iota(jnp.int32, (1, D_ATTN), 1) // HEAD_DIM
        for n in range(n_blk):
            k_mean = km_ref[n]
            for h in range(N_HEADS):
                kmbd_scr[h * n_blk + n:h * n_blk + n + 1, :] = jnp.where(lane_head == h, k_mean, 0.0)

    def scaled_q():
        return (qt_ref[...].astype(F32) * SCALE).astype(BF16)

    def select_blocks():
        gate = jnp.dot(kmbd_scr[...].astype(BF16), scaled_q(), preferred_element_type=F32)
        blk = lax.broadcasted_iota(jnp.int32, (n_blk, tq), 0)
        for h in range(N_HEADS):
            g = gate[h * n_blk:(h + 1) * n_blk, :]
            beaten = jnp.zeros((n_blk, tq), jnp.int32)
            for m in range(n_blk):
                gm = g[m:m + 1, :]
                wins = (gm > g) | ((gm == g) & (m < blk))
                beaten = beaten + jnp.where(wins, 1, 0) * (m < qt).astype(jnp.int32)
            selected = (blk < qt) & (beaten < MOBA_TOPK)
            far = jnp.where(blk < qt - 1, tab_ref[N_BUCKETS - 1, h], 0.0)
            neg_scr[h] = jnp.where(selected, far, NEG_INF)

    @pl.when(scoring)
    def _():
        q_t = scaled_q()
        half = lax.broadcasted_iota(jnp.int32, (2 * HEAD_DIM, tq), 0) // HEAD_DIM
        for h in range(N_HEADS):
            pair = h // 2
            q_pair = q_t[pair * 2 * HEAD_DIM:(pair + 1) * 2 * HEAD_DIM, :]
            qh_scr[h] = jnp.where(half == h % 2, q_pair, jnp.zeros_like(q_pair))
            m_scr[h] = jnp.full(m_scr.shape[1:], NEG_INF, F32)

    def scores(h, n):
        pair = h // 2
        k_pair = kb_scr[n, :, pair * 2 * HEAD_DIM:(pair + 1) * 2 * HEAD_DIM]
        return jnp.dot(k_pair, qh_scr[h], preferred_element_type=F32)

    def sublane_groups(a):
        return a.reshape(a.shape[0] // SUBLANES, SUBLANES, a.shape[1])

    n_group = s_scr.shape[0]
    for g in range(N_HEADS // n_group):
        heads = range(g * n_group, (g + 1) * n_group)

        def extra_bias(h, n, kind):
            if kind == "own":
                return bias_ref[h, 0]
            row = neg_scr[h, pl.ds(n, 1), :]
            return bias_ref[h, 1] + row if kind == "prev" else row

        def score_blocks(blocks):
            for h in heads:
                part = m_scr[h]
                for n, kind in blocks:
                    s = (scores(h, n) + extra_bias(h, n, kind)) * LOG2_E
                    s_scr[h % n_group, n] = s
                    part = jnp.maximum(part, jnp.max(sublane_groups(s), axis=0))
                m_scr[h] = part

        n_far = qt - 1

        @pl.when(scoring & (qt == 0))
        def _():
            score_blocks([(qt, "own")])

        @pl.when(scoring & (qt >= 1) & (n_far % 2 == 0))
        def _(g=g):
            score_blocks([(qt, "own")])
            if g == 0:
                select_blocks()
            score_blocks([(qt - 1, "prev")])

        @pl.when(scoring & (qt >= 1) & (n_far % 2 == 1))
        def _(g=g):
            score_blocks([(qt, "own")])
            if g == 0:
                select_blocks()
            score_blocks([(qt - 1, "prev"), (qt - 2, "far")])

        def far_body(i, carry):
            score_blocks([(2 * i, "far"), (2 * i + 1, "far")])
            return carry

        lax.fori_loop(0, jnp.where(scoring, jnp.maximum(n_far, 0) // 2, 0), far_body, 0)

        m_fin = {h: jnp.max(m_scr[h], axis=0, keepdims=True) for h in heads}

        @pl.when(weighting)
        def _():
            for h in heads:
                l_scr[h] = jnp.zeros(l_scr.shape[1:], F32)
                acc_scr[h * HEAD_DIM:(h + 1) * HEAD_DIM, :] = jnp.zeros((HEAD_DIM, tq), F32)

        def weight_blocks(blocks):
            for h in heads:
                rows = slice(h * HEAD_DIM, (h + 1) * HEAD_DIM)
                l_part = l_scr[h]
                acc = acc_scr[rows, :]
                for n in blocks:
                    p = jnp.exp2(s_scr[h % n_group, n] - m_fin[h])
                    l_part = l_part + jnp.sum(sublane_groups(p), axis=0)
                    v_t = vb_scr[n, rows, :]
                    acc = acc + jnp.dot(v_t, p.astype(BF16), preferred_element_type=F32)
                l_scr[h] = l_part
                acc_scr[rows, :] = acc

        n_all = qt + 1

        @pl.when(weighting & (n_all % 2 == 1))
        def _():
            weight_blocks([qt])

        def weight_body(i, carry):
            weight_blocks([2 * i, 2 * i + 1])
            return carry

        lax.fori_loop(0, jnp.where(weighting, n_all // 2, 0), weight_body, 0)

        @pl.when(weighting)
        def _():
            for h in heads:
                rows = slice(h * HEAD_DIM, (h + 1) * HEAD_DIM)
                acc_scr[rows, :] = acc_scr[rows, :] / jnp.sum(l_scr[h], axis=0, keepdims=True)

    @pl.when(weighting)
    def _():
        o_ref[...] = acc_scr[...].T.astype(o_ref.dtype)


def _sample_attention(step, n_steps, q_ref, kn_ref, vn_ref, sb_ref, k_pages, v_pages, o_ref,
                      g_scr, m_scr, l_scr, o_scr, *, pages_per_block, n_blk):
    blocks_per_step = len(k_pages) // pages_per_block
    t_new = kn_ref.shape[0]
    rows = q_ref.shape[0]
    page = k_pages[0].shape[-1]

    row_head = lax.broadcasted_iota(jnp.int32, (rows, D_ATTN), 0) // t_new
    lane_head = lax.broadcasted_iota(jnp.int32, (rows, D_ATTN), 1) // HEAD_DIM
    own_lanes = row_head == lane_head
    q_rows = jnp.where(own_lanes, q_ref[...] * SCALE, 0.0)
    q_bf = q_rows.astype(BF16)
    lane = lax.broadcasted_iota(jnp.int32, (rows, LANES), 1)

    @pl.when(step == 0)
    def _():
        g_scr[...] = jnp.full(g_scr.shape, NEG_INF, F32)
        m_scr[...] = jnp.full(m_scr.shape, NEG_INF, F32)
        l_scr[...] = jnp.zeros(l_scr.shape, F32)

    def block_t(pages, jb):
        parts = [pages[jb * pages_per_block + jp][...].reshape(D_ATTN, page) for jp in range(pages_per_block)]
        return jnp.concatenate(parts, axis=1).astype(BF16)

    def fold_pages(a):
        return sum(a[:, jp * page:(jp + 1) * page] for jp in range(pages_per_block))

    far_bias = sb_ref[:, MOBA_BLOCK + LANES:MOBA_BLOCK + LANES + 1]
    g_all, m_all, l_all = g_scr[...], m_scr[...], l_scr[...]
    blocks = [step * blocks_per_step + jb for jb in range(blocks_per_step)]
    scores = [jnp.dot(q_bf, block_t(k_pages, jb), preferred_element_type=F32)
              for jb in range(blocks_per_step)]
    probs = []
    for n, s in zip(blocks, scores):
        g_all = jnp.where(lane == n, jnp.sum(fold_pages(s), axis=1, keepdims=True), g_all)
        newest = (jnp.zeros(s.shape, jnp.int32) + n) == n_blk - 1
        s = s + jnp.where(newest, sb_ref[:, 0:MOBA_BLOCK], far_bias)
        m_blk = jnp.max(s, axis=1, keepdims=True)
        p = jnp.exp(s - m_blk)
        l_all = jnp.where(lane == n, jnp.sum(fold_pages(p), axis=1, keepdims=True), l_all)
        m_all = jnp.where(lane == n, m_blk, m_all)
        probs.append(p.astype(BF16))
    g_scr[...], m_scr[...], l_scr[...] = g_all, m_all, l_all
    for jb, (n, p) in enumerate(zip(blocks, probs)):
        o_scr[n] = lax.dot_general(p, block_t(v_pages, jb), NT_DIMS, preferred_element_type=F32)

    @pl.when(step == n_steps - 1)
    def _():
        g = g_all
        beaten = jnp.zeros(g.shape, jnp.int32)
        for m in range(n_blk):
            gm = g[:, m:m + 1]
            beaten = beaten + jnp.where((gm > g) | ((gm == g) & (m < lane)), 1, 0)
        selected = (beaten < MOBA_TOPK) & (lane < n_blk)
        m_sel = jnp.where(selected, m_scr[...], NEG_INF)

        s_loc = jnp.full((rows, LANES), NEG_INF, F32)
        for t in range(t_new):
            col = jnp.sum(q_rows * kn_ref[t:t + 1, :], axis=1, keepdims=True)
            s_loc = jnp.where(lane == t, col, s_loc)
        s_loc = s_loc + sb_ref[:, MOBA_BLOCK:MOBA_BLOCK + LANES]

        m_tot = jnp.maximum(jnp.max(m_sel, axis=1, keepdims=True), jnp.max(s_loc, axis=1, keepdims=True))
        w_blk = jnp.where(selected, jnp.exp(m_sel - m_tot), 0.0)
        p_loc = jnp.exp(s_loc - m_tot)
        denom = jnp.sum(w_blk * l_all, axis=1, keepdims=True) + jnp.sum(p_loc, axis=1, keepdims=True)
        acc = jnp.zeros((rows, D_ATTN), F32)
        for n in range(n_blk):
            acc = acc + w_blk[:, n:n + 1] * o_scr[n]
        for t in range(t_new):
            acc = acc + p_loc[:, t:t + 1] * vn_ref[t:t + 1, :]
        acc = jnp.where(own_lanes, acc / denom, 0.0)
        folded = acc
        for h in range(1, N_HEADS):
            folded = folded + pltpu.roll(acc, rows - h * t_new, 0)
        o_ref[...] = folded[:t_new, :]


def _attention_kernel(pt_ref, tab_ref, qt_ref, kb_ref, vb_ref, km_ref, bias_ref, sq_ref, skn_ref, svn_ref, sb_ref,
                      ck_hbm, cv_hbm, o_prompt, o_sample, *scratch,
                      layer, n_windows, pages_per_block, n_blk_sample, steps_per_seq):
    prompt_scratch, sample_scratch = scratch[:7], scratch[7:11]
    kbuf, vbuf, sem = scratch[11:]
    qt = pl.program_id(1)
    sub = pl.program_id(2)
    linear = (pl.program_id(0) * pl.num_programs(1) + qt) * pl.num_programs(2) + sub
    n_total = pl.num_programs(0) * pl.num_programs(1) * pl.num_programs(2)

    def page_copy(hbm, buf, which, slot, w, page_id):
        return pltpu.make_async_copy(hbm.at[layer, page_id], buf.at[slot, w], sem.at[which, slot])

    def start_pages(lin, slot):
        seq = lin // steps_per_seq
        first = (lin % steps_per_seq) * n_windows
        for w in range(n_windows):
            page_id = pt_ref[seq, first + w]
            page_copy(ck_hbm, kbuf, 0, slot, w, page_id).start()
            page_copy(cv_hbm, vbuf, 1, slot, w, page_id).start()

    @pl.when(linear == 0)
    def _():
        start_pages(linear, 0)

    @pl.when(linear + 1 < n_total)
    def _():
        start_pages(linear + 1, (linear + 1) % 2)

    slot = linear % 2
    for w in range(n_windows):
        page_copy(ck_hbm, kbuf, 0, slot, w, 0).wait()
        page_copy(cv_hbm, vbuf, 1, slot, w, 0).wait()
    k_pages = [kbuf.at[slot, w] for w in range(n_windows)]
    v_pages = [vbuf.at[slot, w] for w in range(n_windows)]

    _sample_attention(linear % steps_per_seq, steps_per_seq, sq_ref, skn_ref, svn_ref, sb_ref, k_pages, v_pages,
                      o_sample, *sample_scratch, pages_per_block=pages_per_block, n_blk=n_blk_sample)
    _prompt_attention(qt, sub, tab_ref, qt_ref, kb_ref, vb_ref, km_ref, bias_ref, o_prompt, *prompt_scratch)


def _attention(rel_bias, q_t, k_blocks, vt_blocks, k_means, bias_tiles,
               q, k_new, v_new, cache_kt, cache_vt, layer, page_table, sbias, blocks_per_step):
    b, _, s = q_t.shape
    assert s % MOBA_BLOCK == 0
    n_blk = s // MOBA_BLOCK
    tq = MOBA_BLOCK
    n_qt = s // tq
    n_pass = 2

    db, t_new, _ = q.shape
    page = cache_kt.shape[-1]
    n_pages = page_table.shape[1]
    pages_per_block = MOBA_BLOCK // page
    assert page == LANES and MOBA_BLOCK % page == 0
    assert (n_pages * page) % MOBA_BLOCK == 0 and t_new <= LANES
    n_blk_s = n_pages // pages_per_block
    assert MOBA_TOPK <= n_blk_s <= LANES and n_blk_s % blocks_per_step == 0
    assert MOBA_BLOCK + 1 >= FAR_DISTANCE
    steps_per_seq = n_blk_s // blocks_per_step
    assert b * n_qt * n_pass == db * steps_per_seq
    rows = N_HEADS * t_new
    q_rows = jnp.tile(q, (1, N_HEADS, 1))
    pages_per_step = blocks_per_step * pages_per_block

    def linear(i, j, k):
        return (i * n_qt + j) * n_pass + k

    per_batch = lambda i, j, k, pt: (i, 0, 0, 0)
    per_seq = lambda i, j, k, pt: (linear(i, j, k) // steps_per_seq, 0, 0)
    tok_spec = pl.BlockSpec((None, t_new, D_ATTN), per_seq)
    grid_spec = pltpu.PrefetchScalarGridSpec(
        num_scalar_prefetch=1,
        grid=(b, n_qt, n_pass),
        in_specs=[pl.BlockSpec(memory_space=pltpu.SMEM),
                  pl.BlockSpec((None, D_ATTN, tq), lambda i, j, k, pt: (i, 0, j)),
                  pl.BlockSpec((None, n_blk, MOBA_BLOCK, D_ATTN), per_batch),
                  pl.BlockSpec((None, n_blk, D_ATTN, MOBA_BLOCK), per_batch),
                  pl.BlockSpec((None, n_blk, 1, D_ATTN), per_batch),
                  pl.BlockSpec(bias_tiles.shape, lambda i, j, k, pt: (0, 0, 0, 0)),
                  pl.BlockSpec((None, rows, D_ATTN), per_seq), tok_spec, tok_spec,
                  pl.BlockSpec(sbias.shape, lambda i, j, k, pt: (0, 0)),
                  pl.BlockSpec(memory_space=pl.ANY), pl.BlockSpec(memory_space=pl.ANY)],
        out_specs=[pl.BlockSpec((None, tq, D_ATTN), lambda i, j, k, pt: (i, j, 0)), tok_spec],
        scratch_shapes=[pltpu.VMEM((N_HEADS * n_blk, D_ATTN), F32),
                        pltpu.VMEM((N_HEADS, n_blk, tq), F32),
                        pltpu.VMEM((N_HEADS, 2 * HEAD_DIM, tq), BF16),
                        pltpu.VMEM((N_HEADS, n_blk, MOBA_BLOCK, tq), F32),
                        pltpu.VMEM((N_HEADS, SUBLANES, tq), F32),
                        pltpu.VMEM((N_HEADS, SUBLANES, tq), F32),
                        pltpu.VMEM((D_ATTN, tq), F32),
                        pltpu.VMEM((rows, LANES), F32), pltpu.VMEM((rows, LANES), F32), pltpu.VMEM((rows, LANES), F32),
                        pltpu.VMEM((n_blk_s, rows, D_ATTN), F32),
                        pltpu.VMEM((2, pages_per_step, N_HEADS, HEAD_DIM, page), F32),
                        pltpu.VMEM((2, pages_per_step, N_HEADS, HEAD_DIM, page), F32),
                        pltpu.SemaphoreType.DMA((2, 2))],
    )
    return pl.pallas_call(
        functools.partial(_attention_kernel, layer=layer, n_windows=pages_per_step,
                          pages_per_block=pages_per_block, n_blk_sample=n_blk_s, steps_per_seq=steps_per_seq),
        grid_spec=grid_spec,
        out_shape=[jax.ShapeDtypeStruct((b, s, D_ATTN), BF16), jax.ShapeDtypeStruct((db, t_new, D_ATTN), F32)],
        compiler_params=pltpu.CompilerParams(dimension_semantics=("arbitrary", "arbitrary", "arbitrary"),
                                             vmem_limit_bytes=VMEM_LIMIT_BYTES),
        name="moba_attention",
    )(page_table, rel_bias, q_t, k_blocks, vt_blocks, k_means, bias_tiles, q_rows, k_new, v_new, sbias,
      cache_kt, cache_vt)


def _merge_kernel(x_ref, attn_ref, ga_ref, u_ref, vn_ref, gb_ref, ma_ref, mb_ref,
                  mix_ref, sb_ref, wpa_ref, wpb_ref, wo_ref, fg_ref, o_ref, *, final):
    tm = x_ref.shape[0]
    d_sgu = u_ref.shape[-1]
    group_dim = d_sgu // N_GROUPS
    first_of_pair = lax.broadcasted_iota(jnp.int32, (CHUNK, 2 * group_dim), 1) < group_dim
    z_chunks = []
    for c in range(tm // CHUNK):
        vn = vn_ref[c * CHUNK:(c + 1) * CHUNK, :].astype(BF16)
        z_pairs = []
        for j in range(N_GROUPS // 2):
            v_pair = vn[:, j * 2 * group_dim:(j + 1) * 2 * group_dim]
            z_a = jnp.dot(mix_ref[2 * j], v_pair, preferred_element_type=F32)
            z_b = jnp.dot(mix_ref[2 * j + 1], v_pair, preferred_element_type=F32)
            z_pairs.append(jnp.where(first_of_pair, z_a, z_b))
        z_chunks.append(jnp.concatenate(z_pairs, axis=1) + sb_ref[...])
    z = jnp.concatenate(z_chunks, axis=0) if len(z_chunks) > 1 else z_chunks[0]
    y_b = (u_ref[...].astype(F32) * z) * _silu(gb_ref[...].astype(F32))
    y_a = attn_ref[...].astype(F32) * _silu(ga_ref[...].astype(F32))
    p_a = jnp.dot(y_a.astype(BF16), wpa_ref[...], preferred_element_type=F32)
    p_b = jnp.dot(y_b.astype(BF16), wpb_ref[...], preferred_element_type=F32)
    mixed = (jax.nn.sigmoid(ma_ref[...].astype(F32)) * p_a
             + jax.nn.sigmoid(mb_ref[...].astype(F32)) * p_b)
    out = x_ref[...] + jnp.dot(mixed.astype(BF16), wo_ref[...], preferred_element_type=F32)
    if final:
        out = out * lax.rsqrt(jnp.mean(out * out, axis=-1, keepdims=True) + EPS) * fg_ref[...]
    o_ref[...] = out


def _merge(x, attn, ga, u, vn, gb, ma, mb, mix_bf, sgu_bias, wpa_bf, wpb_bf, wo_bf, final_g, tm, final):
    n_tok, d_model = x.shape
    d_sgu = u.shape[-1]
    assert n_tok % tm == 0 and tm % CHUNK == 0
    row = lambda i: (i, 0)
    fixed2 = lambda i: (0, 0)
    narrow = pl.BlockSpec((tm, d_sgu), row)
    wide = pl.BlockSpec((tm, d_model), row)
    return pl.pallas_call(
        functools.partial(_merge_kernel, final=final),
        grid=(n_tok // tm,),
        in_specs=[wide, pl.BlockSpec((tm, D_ATTN), row), pl.BlockSpec((tm, D_ATTN), row),
                  narrow, narrow, narrow, wide, wide,
                  pl.BlockSpec(mix_bf.shape, lambda i: (0, 0, 0)),
                  pl.BlockSpec(sgu_bias.shape, fixed2),
                  pl.BlockSpec(wpa_bf.shape, fixed2),
                  pl.BlockSpec(wpb_bf.shape, fixed2),
                  pl.BlockSpec(wo_bf.shape, fixed2),
                  pl.BlockSpec((1, d_model), fixed2)],
        out_specs=wide,
        out_shape=jax.ShapeDtypeStruct((n_tok, d_model), F32),
        compiler_params=pltpu.CompilerParams(dimension_semantics=("arbitrary",),
                                             vmem_limit_bytes=VMEM_LIMIT_BYTES),
        name="merge",
    )(x, attn, ga, u, vn, gb, ma, mb, mix_bf, sgu_bias, wpa_bf, wpb_bf, wo_bf, final_g.reshape(1, d_model))


PROMPT_TOKEN_TILE = 256
MERGE_TOKEN_TILE = 512
SAMPLE_BLOCKS_PER_STEP = 8


def kernel(x_prompt, x_sample, cache_k, cache_v, page_table, rel_bias, norm_g, w_in, sgu_ln_g, sgu_ln_b,
           w_s, b_s, w_pa, w_pb, w_o, final_norm_g):
    b, s, d_model = x_prompt.shape
    db, t_new, _ = x_sample.shape
    depth = w_in.shape[0]
    d_sgu = sgu_ln_g.shape[-1]
    group_dim = d_sgu // N_GROUPS
    assert s % CHUNK == 0 and (db * t_new) % CHUNK == 0 and CHUNK % t_new == 0
    assert cache_k.shape[3] == N_HEADS and cache_k.shape[4] == HEAD_DIM

    bias_tiles = _prompt_bias_tiles(rel_bias)
    sbias = _sample_bias(rel_bias, t_new)
    cache_kt = jnp.transpose(cache_k, (0, 1, 3, 4, 2))
    cache_vt = jnp.transpose(cache_v, (0, 1, 3, 4, 2))

    xp = x_prompt.reshape(b * s, d_model)
    xs = x_sample.reshape(db * t_new, d_model)
    kv_stack = None
    k_s, v_s, sgu_rows = [], [], []
    for l in range(depth):
        final = l == depth - 1
        wqkv = w_in[l][:, :3 * D_ATTN]
        wqkv_bf, wqkv_t_bf = wqkv.astype(BF16), wqkv.T.astype(BF16)
        w_bf = w_in[l][:, 3 * D_ATTN:].astype(BF16)
        wpa_bf, wpb_bf, wo_bf = w_pa[l].astype(BF16), w_pb[l].astype(BF16), w_o[l].astype(BF16)
        mix_p = jnp.tril(w_s[l]).astype(BF16)
        bias_p = jnp.repeat(b_s[l].T, group_dim, axis=1)
        corner = jnp.tril(w_s[l][:, :t_new, :t_new])
        eye = jnp.eye(CHUNK // t_new, dtype=F32)
        mix_s = jnp.einsum("ab,gts->gatbs", eye, corner).reshape(N_GROUPS, CHUNK, CHUNK).astype(BF16)
        bias_s = jnp.tile(jnp.repeat(b_s[l][:, :t_new].T, group_dim, axis=1), (CHUNK // t_new, 1))

        q_t, k_all, v_all, ga, u, vn, gb, ma, mb, k_blocks, vt_blocks, k_means = _inproj(
            xp, norm_g[l], wqkv_t_bf, w_bf, sgu_ln_g[l], sgu_ln_b[l], PROMPT_TOKEN_TILE,
            stack_shape=(depth, b, D_ATTN, s), kv_stack=kv_stack, layer=l, seq_len=s)
        kv_stack = (k_all, v_all)
        q, k, v, ga_s, u_s, vn_s, gb_s, ma_s, mb_s = _inproj(
            xs, norm_g[l], wqkv_bf, w_bf, sgu_ln_g[l], sgu_ln_b[l], CHUNK)

        attn, attn_s = _attention(
            rel_bias, q_t, k_blocks, vt_blocks, k_means, bias_tiles,
            q.reshape(db, t_new, D_ATTN), k.reshape(db, t_new, D_ATTN), v.reshape(db, t_new, D_ATTN),
            cache_kt, cache_vt, l, page_table, sbias, SAMPLE_BLOCKS_PER_STEP)

        xp = _merge(xp, attn.reshape(b * s, D_ATTN), ga, u, vn, gb, ma, mb, mix_p, bias_p,
                    wpa_bf, wpb_bf, wo_bf, final_norm_g, MERGE_TOKEN_TILE, final)
        xs = _merge(xs, attn_s.reshape(db * t_new, D_ATTN), ga_s, u_s, vn_s, gb_s, ma_s, mb_s, mix_s, bias_s,
                    wpa_bf, wpb_bf, wo_bf, final_norm_g, CHUNK, final)
        k_s.append(k.reshape(db, t_new, N_HEADS, HEAD_DIM))
        v_s.append(v.reshape(db, t_new, N_HEADS, HEAD_DIM))
        sgu_rows.append(vn_s.reshape(db, t_new, d_sgu))

    k_prompt = jnp.transpose(k_all.reshape(depth, b, N_HEADS, HEAD_DIM, s), (0, 1, 4, 2, 3))
    v_prompt = jnp.transpose(v_all.reshape(depth, b, N_HEADS, HEAD_DIM, s), (0, 1, 4, 2, 3))
    return (xp.reshape(b, s, d_model), xs.reshape(db, t_new, d_model),
            k_prompt, v_prompt, jnp.stack(k_s), jnp.stack(v_s), jnp.stack(sgu_rows))
```

```python
import functools
import math

import numpy as np
import jax
import jax.numpy as jnp
from jax import lax
from jax.experimental import pallas as pl
from jax.experimental.pallas import tpu as pltpu

N_HEADS = 8
HEAD_DIM = 64
D_ATTN = N_HEADS * HEAD_DIM
MOBA_BLOCK = 256
MOBA_TOPK = 3
N_GROUPS = 8
CHUNK = 128
N_BUCKETS = 32
MAX_DISTANCE = 128
EPS = 1e-6
LN_EPS = 1e-5
SCALE = HEAD_DIM ** -0.5
LOG2_E = math.log2(math.e)

F32 = jnp.float32
BF16 = jnp.bfloat16
NEG_INF = float("-inf")
NT_DIMS = (((1,), (1,)), ((), ()))

LANES = 128
SUBLANES = 8
VMEM_LIMIT_BYTES = 56 * 1024 * 1024


def _bucket_thresholds():
    max_exact = N_BUCKETS // 2
    d = np.arange(max_exact, 4 * MAX_DISTANCE, dtype=np.float64)
    log_b = max_exact + (np.log(d / max_exact) / math.log(MAX_DISTANCE / max_exact) * (N_BUCKETS - max_exact)).astype(np.int64)
    bucket = np.minimum(log_b, N_BUCKETS - 1)
    return [int(d[np.argmax(bucket >= b)]) for b in range(max_exact + 1, N_BUCKETS)]


BUCKET_THRESHOLDS = _bucket_thresholds()
FAR_DISTANCE = BUCKET_THRESHOLDS[-1]


def _rel_bucket(dist):
    max_exact = N_BUCKETS // 2
    log_b = jnp.full(dist.shape, max_exact, jnp.int32)
    for thr in BUCKET_THRESHOLDS:
        log_b = log_b + jnp.where(dist >= thr, 1, 0)
    return jnp.where(dist < max_exact, dist, log_b)


def _gelu_tanh(x):
    return 0.5 * x * (1.0 + jnp.tanh(math.sqrt(2.0 / math.pi) * (x + 0.044715 * (x * x * x))))


def _silu(x):
    return x * jax.nn.sigmoid(x)


def _prompt_bias_kernel(tab_ref, o_ref):
    h = pl.program_id(0)
    c = pl.program_id(1)
    shape = o_ref.shape
    key = lax.broadcasted_iota(jnp.int32, shape, 0)
    qry = lax.broadcasted_iota(jnp.int32, shape, 1)
    dist = c * MOBA_BLOCK + qry - key
    bucket = _rel_bucket(dist)
    val = jnp.full(shape, tab_ref[N_BUCKETS - 1, h], F32)
    for b in range(N_BUCKETS - 1):
        val = jnp.where(bucket == b, tab_ref[b, h], val)
    o_ref[...] = jnp.where(dist < 0, NEG_INF, val)


def _prompt_bias_tiles(rel_bias):
    assert MOBA_BLOCK + 1 >= FAR_DISTANCE
    return pl.pallas_call(
        _prompt_bias_kernel,
        grid=(N_HEADS, 2),
        in_specs=[pl.BlockSpec(memory_space=pltpu.SMEM)],
        out_specs=pl.BlockSpec((None, None, MOBA_BLOCK, MOBA_BLOCK), lambda h, c: (h, c, 0, 0)),
        out_shape=jax.ShapeDtypeStruct((N_HEADS, 2, MOBA_BLOCK, MOBA_BLOCK), F32),
        compiler_params=pltpu.CompilerParams(dimension_semantics=("arbitrary", "arbitrary")),
        name="prompt_bias_tiles",
    )(rel_bias)


def _sample_bias_kernel(tab_ref, o_ref, *, t_new):
    shape = o_ref.shape
    row = lax.broadcasted_iota(jnp.int32, shape, 0)
    lane = lax.broadcasted_iota(jnp.int32, shape, 1)
    t = row % t_new
    dist = jnp.where(lane < MOBA_BLOCK, MOBA_BLOCK + t - lane,
                     jnp.where(lane < MOBA_BLOCK + LANES, t - (lane - MOBA_BLOCK), FAR_DISTANCE))
    bucket = _rel_bucket(dist)
    tab = tab_ref[...]
    val = jnp.broadcast_to(tab[:, N_BUCKETS - 1:N_BUCKETS], shape)
    for b in range(N_BUCKETS - 1):
        val = jnp.where(bucket == b, tab[:, b:b + 1], val)
    o_ref[...] = jnp.where(dist < 0, NEG_INF, val)


def _sample_bias(rel_bias, t_new):
    rows = N_HEADS * t_new
    tab_rows = jnp.repeat(rel_bias.T, t_new, axis=0)
    return pl.pallas_call(
        functools.partial(_sample_bias_kernel, t_new=t_new),
        out_shape=jax.ShapeDtypeStruct((rows, MOBA_BLOCK + 2 * LANES), F32),
        name="sample_bias",
    )(tab_rows)


def _inproj_kernel(*refs, transposed_qkv, n_aliased):
    x_ref, g_ref, wqkv_ref, w_ref, lng_ref, lnb_ref = refs[:6]
    q_ref, k_ref, v_ref, ga_ref, u_ref, vn_ref, gb_ref, ma_ref, mb_ref = refs[6 + n_aliased:6 + n_aliased + 9]
    x = x_ref[...]
    d_model = x.shape[-1]
    d_sgu = u_ref.shape[-1]
    h = (x * lax.rsqrt(jnp.mean(x * x, axis=-1, keepdims=True) + EPS) * g_ref[...]).astype(BF16)

    if transposed_qkv:
        qkv_t = lax.dot_general(wqkv_ref[...], h, NT_DIMS, preferred_element_type=F32)
        q_ref[...] = qkv_t[0:D_ATTN].astype(q_ref.dtype)
        for ref, rows in ((k_ref, qkv_t[D_ATTN:2 * D_ATTN]), (v_ref, qkv_t[2 * D_ATTN:3 * D_ATTN])):
            if len(ref.shape) == 2:
                ref[...] = rows
            else:
                ref[0] = rows
                for later in range(1, ref.shape[0]):
                    ref[later] = jnp.zeros_like(rows)
        kb_ref, vb_ref, km_ref = refs[6 + n_aliased + 9:]
        k_rows = qkv_t[D_ATTN:2 * D_ATTN].T
        kb_ref[...] = k_rows.astype(BF16)
        vb_ref[...] = qkv_t[2 * D_ATTN:3 * D_ATTN].astype(BF16)
        km_ref[...] = jnp.sum(k_rows, axis=0, keepdims=True) * (1.0 / k_rows.shape[0])
    else:
        qkv = jnp.dot(h, wqkv_ref[...], preferred_element_type=F32)
        q_ref[...] = qkv[:, 0:D_ATTN]
        k_ref[...] = qkv[:, D_ATTN:2 * D_ATTN]
        v_ref[...] = qkv[:, 2 * D_ATTN:3 * D_ATTN]

    def seg(lo, width):
        return jnp.dot(h, w_ref[:, lo:lo + width], preferred_element_type=F32)

    act = ga_ref.dtype
    ga_ref[...] = seg(0, D_ATTN).astype(act)
    base = D_ATTN
    u_ref[...] = _gelu_tanh(seg(base, d_sgu)).astype(act)
    vs = _gelu_tanh(seg(base + d_sgu, d_sgu))
    mu = jnp.mean(vs, axis=-1, keepdims=True)
    var = jnp.mean(jnp.square(vs - mu), axis=-1, keepdims=True)
    vn_ref[...] = ((vs - mu) * lax.rsqrt(var + LN_EPS) * lng_ref[...] + lnb_ref[...]).astype(act)
    gb_ref[...] = seg(base + 2 * d_sgu, d_sgu).astype(act)
    ma_ref[...] = seg(base + 3 * d_sgu, d_model).astype(act)
    mb_ref[...] = seg(base + 3 * d_sgu + d_model, d_model).astype(act)


def _inproj(x, norm_g, wqkv_bf, w_bf, ln_g, ln_b, tm, stack_shape=None, kv_stack=None, layer=None, seq_len=None):
    n_tok, d_model = x.shape
    d_sgu = ln_g.shape[-1]
    assert n_tok % tm == 0
    widths = [D_ATTN] + [d_sgu] * 3 + [d_model] * 2
    assert sum(widths) == w_bf.shape[1]
    row = lambda i: (i, 0)
    fixed = lambda i: (0, 0)
    transposed = stack_shape is not None
    in_specs = [pl.BlockSpec((tm, d_model), row),
                pl.BlockSpec((1, d_model), fixed),
                pl.BlockSpec(wqkv_bf.shape, fixed),
                pl.BlockSpec(w_bf.shape, fixed),
                pl.BlockSpec((1, d_sgu), fixed),
                pl.BlockSpec((1, d_sgu), fixed)]
    args = [x, norm_g.reshape(1, d_model), wqkv_bf, w_bf, ln_g.reshape(1, d_sgu), ln_b.reshape(1, d_sgu)]
    rest_specs = [pl.BlockSpec((tm, w), row) for w in widths]
    act_dtype = BF16 if transposed else F32
    rest_shapes = [jax.ShapeDtypeStruct((n_tok, w), act_dtype) for w in widths]
    aliases = {}
    if transposed:
        assert seq_len % tm == 0 and n_tok % seq_len == 0
        tiles = seq_len // tm
        batch = n_tok // seq_len
        if kv_stack is not None:
            in_specs += [pl.BlockSpec(memory_space=pl.ANY)] * 2
            args += list(kv_stack)
            aliases = {6: 1, 7: 2}
            stack_spec = pl.BlockSpec((None, None, D_ATTN, tm), lambda i: (layer, i // tiles, 0, i % tiles))
        else:
            assert layer == 0
            stack_spec = pl.BlockSpec((stack_shape[0], None, D_ATTN, tm), lambda i: (0, i // tiles, 0, i % tiles))
        qkv_specs = [pl.BlockSpec((None, D_ATTN, tm), lambda i: (i // tiles, 0, i % tiles)), stack_spec, stack_spec]
        qkv_shapes = [jax.ShapeDtypeStruct((batch, D_ATTN, seq_len), BF16),
                      jax.ShapeDtypeStruct(stack_shape, F32), jax.ShapeDtypeStruct(stack_shape, F32)]
    else:
        qkv_specs = [pl.BlockSpec((tm, D_ATTN), row)] * 3
        qkv_shapes = [jax.ShapeDtypeStruct((n_tok, D_ATTN), F32)] * 3
    block_specs, block_shapes = [], []
    if transposed:
        assert tm == MOBA_BLOCK
        blk_map = lambda i: (i // tiles, i % tiles, 0, 0)
        block_specs = [pl.BlockSpec((None, None, tm, D_ATTN), blk_map),
                       pl.BlockSpec((None, None, D_ATTN, tm), blk_map),
                       pl.BlockSpec((None, None, 1, D_ATTN), blk_map)]
        block_shapes = [jax.ShapeDtypeStruct((batch, tiles, tm, D_ATTN), BF16),
                        jax.ShapeDtypeStruct((batch, tiles, D_ATTN, tm), BF16),
                        jax.ShapeDtypeStruct((batch, tiles, 1, D_ATTN), F32)]
    return pl.pallas_call(
        functools.partial(_inproj_kernel, transposed_qkv=transposed, n_aliased=len(aliases)),
        grid=(n_tok // tm,),
        in_specs=in_specs,
        out_specs=qkv_specs + rest_specs + block_specs,
        out_shape=qkv_shapes + rest_shapes + block_shapes,
        input_output_aliases=aliases,
        compiler_params=pltpu.CompilerParams(dimension_semantics=("arbitrary",),
                                             vmem_limit_bytes=VMEM_LIMIT_BYTES),
        name="inproj",
    )(*args)


def _prompt_attention(qt, sub, tab_ref, qt_ref, kb_scr, vb_scr, km_ref, bias_ref, o_ref,
                      kmbd_scr, neg_scr, qh_scr, s_scr, m_scr, l_scr, acc_scr, sample_step=None):
    n_blk = kb_scr.shape[0]
    tq = qt_ref.shape[1]
    scoring = sub == 0
    weighting = sub == 1

    @pl.when(scoring & (qt == 0))
    def _():
        lane_head = lax.broadcasted_iota(jnp.int32, (1, D_ATTN), 1) // HEAD_DIM
        for n in range(n_blk):
            k_mean = km_ref[n]
            for h in range(N_HEADS):
                kmbd_scr[h * n_blk + n:h * n_blk + n + 1, :] = jnp.where(lane_head == h, k_mean, 0.0)

    def scaled_q():
        return (qt_ref[...].astype(F32) * SCALE).astype(BF16)

    def select_blocks():
        gate = jnp.dot(kmbd_scr[...].astype(BF16), scaled_q(), preferred_element_type=F32)
        blk = lax.broadcasted_iota(jnp.int32, (n_blk, tq), 0)
        for h in range(N_HEADS):
            g = gate[h * n_blk:(h + 1) * n_blk, :]
            beaten = jnp.zeros((n_blk, tq), jnp.int32)
            for m in range(n_blk):
                gm = g[m:m + 1, :]
                wins = (gm > g) | ((gm == g) & (m < blk))
                beaten = beaten + jnp.where(wins, 1, 0) * (m < qt).astype(jnp.int32)
            selected = (blk < qt) & (beaten < MOBA_TOPK)
            far = jnp.where(blk < qt - 1, tab_ref[N_BUCKETS - 1, h], 0.0)
            neg_scr[h] = jnp.where(selected, far, NEG_INF)

    @pl.when(scoring)
    def _():
        q_t = scaled_q()
        half = lax.broadcasted_iota(jnp.int32, (2 * HEAD_DIM, tq), 0) // HEAD_DIM
        for h in range(N_HEADS):
            pair = h // 2
            q_pair = q_t[pair * 2 * HEAD_DIM:(pair + 1) * 2 * HEAD_DIM, :]
            qh_scr[h] = jnp.where(half == h % 2, q_pair, jnp.zeros_like(q_pair))
            m_scr[h] = jnp.full(m_scr.shape[1:], NEG_INF, F32)

    def scores(h, n):
        pair = h // 2
        k_pair = kb_scr[n, :, pair * 2 * HEAD_DIM:(pair + 1) * 2 * HEAD_DIM]
        return jnp.dot(k_pair, qh_scr[h], preferred_element_type=F32)

    def sublane_groups(a):
        return a.reshape(a.shape[0] // SUBLANES, SUBLANES, a.shape[1])

    n_group = s_scr.shape[0]
    for g in range(N_HEADS // n_group):
        heads = range(g * n_group, (g + 1) * n_group)

        def extra_bias(h, n, kind):
            if kind == "own":
                return bias_ref[h, 0]
            row = neg_scr[h, pl.ds(n, 1), :]
            return bias_ref[h, 1] + row if kind == "prev" else row

        def score_blocks(blocks):
            for h in heads:
                part = m_scr[h]
                for n, kind in blocks:
                    s = (scores(h, n) + extra_bias(h, n, kind)) * LOG2_E
                    s_scr[h % n_group, n] = s
                    part = jnp.maximum(part, jnp.max(sublane_groups(s), axis=0))
                m_scr[h] = part

        n_far = qt - 1

        @pl.when(scoring & (qt == 0))
        def _():
            score_blocks([(qt, "own")])

        @pl.when(scoring & (qt >= 1) & (n_far % 2 == 0))
        def _(g=g):
            score_blocks([(qt, "own")])
            if g == 0:
                select_blocks()
            score_blocks([(qt - 1, "prev")])

        @pl.when(scoring & (qt >= 1) & (n_far % 2 == 1))
        def _(g=g):
            score_blocks([(qt, "own")])
            if g == 0:
                select_blocks()
            score_blocks([(qt - 1, "prev"), (qt - 2, "far")])

        def far_body(i, carry):
            score_blocks([(2 * i, "far"), (2 * i + 1, "far")])
            return carry

        lax.fori_loop(0, jnp.where(scoring, jnp.maximum(n_far, 0) // 2, 0), far_body, 0)

        m_fin = {h: jnp.max(m_scr[h], axis=0, keepdims=True) for h in heads}

        def weight_blocks(blocks, first=False):
            for h in heads:
                rows = slice(h * HEAD_DIM, (h + 1) * HEAD_DIM)
                l_part = jnp.zeros(l_scr.shape[1:], F32) if first else l_scr[h]
                acc = jnp.zeros((HEAD_DIM, tq), F32) if first else acc_scr[rows, :]
                for n in blocks:
                    p = jnp.exp2(s_scr[h % n_group, n] - m_fin[h])
                    l_part = l_part + jnp.sum(sublane_groups(p), axis=0)
                    v_t = vb_scr[n, rows, :]
                    acc = acc + jnp.dot(v_t, p.astype(BF16), preferred_element_type=F32)
                l_scr[h] = l_part
                acc_scr[rows, :] = acc

        @pl.when(weighting)
        def _(g=g):
            own_block = functools.partial(weight_blocks, [qt], first=True)
            if g == 0 and sample_step is not None:
                sample_step(alongside=own_block)
            else:
                own_block()

        n_rest = qt

        @pl.when(weighting & (n_rest % 2 == 1))
        def _():
            weight_blocks([qt - 1])

        def weight_body(i, carry):
            weight_blocks([2 * i, 2 * i + 1])
            return carry

        lax.fori_loop(0, jnp.where(weighting, n_rest // 2, 0), weight_body, 0)

        @pl.when(weighting)
        def _():
            for h in heads:
                rows = slice(h * HEAD_DIM, (h + 1) * HEAD_DIM)
                acc_scr[rows, :] = acc_scr[rows, :] / jnp.sum(l_scr[h], axis=0, keepdims=True)

    @pl.when(weighting)
    def _():
        o_ref[...] = acc_scr[...].T.astype(o_ref.dtype)


def _sample_attention(step, n_steps, q_ref, kn_ref, vn_ref, sb_ref, k_pages, v_pages, o_ref,
                      g_scr, m_scr, l_scr, o_scr, *, pages_per_block, n_blk, alongside=None):
    blocks_per_step = len(k_pages) // pages_per_block
    t_new = kn_ref.shape[0]
    rows = q_ref.shape[0]
    page = k_pages[0].shape[-1]

    row_head = lax.broadcasted_iota(jnp.int32, (rows, D_ATTN), 0) // t_new
    lane_head = lax.broadcasted_iota(jnp.int32, (rows, D_ATTN), 1) // HEAD_DIM
    own_lanes = row_head == lane_head
    q_rows = jnp.where(own_lanes, q_ref[...] * SCALE, 0.0)
    q_bf = q_rows.astype(BF16)
    lane = lax.broadcasted_iota(jnp.int32, (rows, LANES), 1)

    @pl.when(step == 0)
    def _():
        g_scr[...] = jnp.full(g_scr.shape, NEG_INF, F32)
        m_scr[...] = jnp.full(m_scr.shape, NEG_INF, F32)
        l_scr[...] = jnp.zeros(l_scr.shape, F32)

    def block_t(pages, jb):
        parts = [pages[jb * pages_per_block + jp][...].reshape(D_ATTN, page) for jp in range(pages_per_block)]
        return jnp.concatenate(parts, axis=1).astype(BF16)

    def fold_pages(a):
        return sum(a[:, jp * page:(jp + 1) * page] for jp in range(pages_per_block))

    far_bias = sb_ref[:, MOBA_BLOCK + LANES:MOBA_BLOCK + LANES + 1]
    g_all, m_all, l_all = g_scr[...], m_scr[...], l_scr[...]
    blocks = [step * blocks_per_step + jb for jb in range(blocks_per_step)]
    scores = [jnp.dot(q_bf, block_t(k_pages, jb), preferred_element_type=F32)
              for jb in range(blocks_per_step)]
    probs = []
    for n, s in zip(blocks, scores):
        g_all = jnp.where(lane == n, jnp.sum(fold_pages(s), axis=1, keepdims=True), g_all)
        newest = (jnp.zeros(s.shape, jnp.int32) + n) == n_blk - 1
        s = s + jnp.where(newest, sb_ref[:, 0:MOBA_BLOCK], far_bias)
        m_blk = jnp.max(s, axis=1, keepdims=True)
        p = jnp.exp(s - m_blk)
        l_all = jnp.where(lane == n, jnp.sum(fold_pages(p), axis=1, keepdims=True), l_all)
        m_all = jnp.where(lane == n, m_blk, m_all)
        probs.append(p.astype(BF16))
    g_scr[...], m_scr[...], l_scr[...] = g_all, m_all, l_all
    for jb, (n, p) in enumerate(zip(blocks, probs)):
        o_scr[n] = lax.dot_general(p, block_t(v_pages, jb), NT_DIMS, preferred_element_type=F32)

    if alongside is not None:
        alongside()

    @pl.when(step == n_steps - 1)
    def _():
        g = g_all
        beaten = jnp.zeros(g.shape, jnp.int32)
        for m in range(n_blk):
            gm = g[:, m:m + 1]
            beaten = beaten + jnp.where((gm > g) | ((gm == g) & (m < lane)), 1, 0)
        selected = (beaten < MOBA_TOPK) & (lane < n_blk)
        m_sel = jnp.where(selected, m_scr[...], NEG_INF)

        s_loc = jnp.full((rows, LANES), NEG_INF, F32)
        for t in range(t_new):
            col = jnp.sum(q_rows * kn_ref[t:t + 1, :], axis=1, keepdims=True)
            s_loc = jnp.where(lane == t, col, s_loc)
        s_loc = s_loc + sb_ref[:, MOBA_BLOCK:MOBA_BLOCK + LANES]

        m_tot = jnp.maximum(jnp.max(m_sel, axis=1, keepdims=True), jnp.max(s_loc, axis=1, keepdims=True))
        w_blk = jnp.where(selected, jnp.exp(m_sel - m_tot), 0.0)
        p_loc = jnp.exp(s_loc - m_tot)
        denom = jnp.sum(w_blk * l_all, axis=1, keepdims=True) + jnp.sum(p_loc, axis=1, keepdims=True)
        acc = jnp.zeros((rows, D_ATTN), F32)
        for n in range(n_blk):
            acc = acc + w_blk[:, n:n + 1] * o_scr[n]
        for t in range(t_new):
            acc = acc + p_loc[:, t:t + 1] * vn_ref[t:t + 1, :]
        acc = jnp.where(own_lanes, acc / denom, 0.0)
        folded = acc
        for h in range(1, N_HEADS):
            folded = folded + pltpu.roll(acc, rows - h * t_new, 0)
        o_ref[...] = folded[:t_new, :]


def _attention_kernel(pt_ref, tab_ref, qt_ref, kb_ref, vb_ref, km_ref, bias_ref, sq_ref, skn_ref, svn_ref, sb_ref,
                      ck_hbm, cv_hbm, o_prompt, o_sample, *scratch,
                      layer, n_windows, pages_per_block, n_blk_sample, steps_per_seq):
    prompt_scratch, sample_scratch = scratch[:7], scratch[7:11]
    kbuf, vbuf, sem = scratch[11:]
    qt = pl.program_id(1)
    sub = pl.program_id(2)
    linear = (pl.program_id(0) * pl.num_programs(1) + qt) * pl.num_programs(2) + sub
    n_total = pl.num_programs(0) * pl.num_programs(1) * pl.num_programs(2)

    def page_copy(hbm, buf, which, slot, w, page_id):
        return pltpu.make_async_copy(hbm.at[layer, page_id], buf.at[slot, w], sem.at[which, slot])

    def start_pages(lin, slot):
        seq = lin // steps_per_seq
        first = (lin % steps_per_seq) * n_windows
        for w in range(n_windows):
            page_id = pt_ref[seq, first + w]
            page_copy(ck_hbm, kbuf, 0, slot, w, page_id).start()
            page_copy(cv_hbm, vbuf, 1, slot, w, page_id).start()

    @pl.when(linear == 0)
    def _():
        start_pages(linear, 0)

    @pl.when(linear + 1 < n_total)
    def _():
        start_pages(linear + 1, (linear + 1) % 2)

    slot = linear % 2
    for w in range(n_windows):
        page_copy(ck_hbm, kbuf, 0, slot, w, 0).wait()
        page_copy(cv_hbm, vbuf, 1, slot, w, 0).wait()
    k_pages = [kbuf.at[slot, w] for w in range(n_windows)]
    v_pages = [vbuf.at[slot, w] for w in range(n_windows)]

    def sample_step(alongside=None):
        _sample_attention(linear % steps_per_seq, steps_per_seq, sq_ref, skn_ref, svn_ref, sb_ref, k_pages, v_pages,
                          o_sample, *sample_scratch, pages_per_block=pages_per_block, n_blk=n_blk_sample,
                          alongside=alongside)

    pl.when(sub == 0)(sample_step)
    _prompt_attention(qt, sub, tab_ref, qt_ref, kb_ref, vb_ref, km_ref, bias_ref, o_prompt, *prompt_scratch,
                      sample_step=sample_step)


def _attention(rel_bias, q_t, k_blocks, vt_blocks, k_means, bias_tiles,
               q, k_new, v_new, cache_kt, cache_vt, layer, page_table, sbias, blocks_per_step):
    b, _, s = q_t.shape
    assert s % MOBA_BLOCK == 0
    n_blk = s // MOBA_BLOCK
    tq = MOBA_BLOCK
    n_qt = s // tq
    n_pass = 2

    db, t_new, _ = q.shape
    page = cache_kt.shape[-1]
    n_pages = page_table.shape[1]
    pages_per_block = MOBA_BLOCK // page
    assert page == LANES and MOBA_BLOCK % page == 0
    assert (n_pages * page) % MOBA_BLOCK == 0 and t_new <= LANES
    n_blk_s = n_pages // pages_per_block
    assert MOBA_TOPK <= n_blk_s <= LANES and n_blk_s % blocks_per_step == 0
    assert MOBA_BLOCK + 1 >= FAR_DISTANCE
    steps_per_seq = n_blk_s // blocks_per_step
    assert b * n_qt * n_pass == db * steps_per_seq
    rows = N_HEADS * t_new
    q_rows = jnp.tile(q, (1, N_HEADS, 1))
    pages_per_step = blocks_per_step * pages_per_block

    def linear(i, j, k):
        return (i * n_qt + j) * n_pass + k

    per_batch = lambda i, j, k, pt: (i, 0, 0, 0)
    per_seq = lambda i, j, k, pt: (linear(i, j, k) // steps_per_seq, 0, 0)
    tok_spec = pl.BlockSpec((None, t_new, D_ATTN), per_seq)
    grid_spec = pltpu.PrefetchScalarGridSpec(
        num_scalar_prefetch=1,
        grid=(b, n_qt, n_pass),
        in_specs=[pl.BlockSpec(memory_space=pltpu.SMEM),
                  pl.BlockSpec((None, D_ATTN, tq), lambda i, j, k, pt: (i, 0, j)),
                  pl.BlockSpec((None, n_blk, MOBA_BLOCK, D_ATTN), per_batch),
                  pl.BlockSpec((None, n_blk, D_ATTN, MOBA_BLOCK), per_batch),
                  pl.BlockSpec((None, n_blk, 1, D_ATTN), per_batch),
                  pl.BlockSpec(bias_tiles.shape, lambda i, j, k, pt: (0, 0, 0, 0)),
                  pl.BlockSpec((None, rows, D_ATTN), per_seq), tok_spec, tok_spec,
                  pl.BlockSpec(sbias.shape, lambda i, j, k, pt: (0, 0)),
                  pl.BlockSpec(memory_space=pl.ANY), pl.BlockSpec(memory_space=pl.ANY)],
        out_specs=[pl.BlockSpec((None, tq, D_ATTN), lambda i, j, k, pt: (i, j, 0)), tok_spec],
        scratch_shapes=[pltpu.VMEM((N_HEADS * n_blk, D_ATTN), F32),
                        pltpu.VMEM((N_HEADS, n_blk, tq), F32),
                        pltpu.VMEM((N_HEADS, 2 * HEAD_DIM, tq), BF16),
                        pltpu.VMEM((N_HEADS, n_blk, MOBA_BLOCK, tq), F32),
                        pltpu.VMEM((N_HEADS, SUBLANES, tq), F32),
                        pltpu.VMEM((N_HEADS, SUBLANES, tq), F32),
                        pltpu.VMEM((D_ATTN, tq), F32),
                        pltpu.VMEM((rows, LANES), F32), pltpu.VMEM((rows, LANES), F32), pltpu.VMEM((rows, LANES), F32),
                        pltpu.VMEM((n_blk_s, rows, D_ATTN), F32),
                        pltpu.VMEM((2, pages_per_step, N_HEADS, HEAD_DIM, page), F32),
                        pltpu.VMEM((2, pages_per_step, N_HEADS, HEAD_DIM, page), F32),
                        pltpu.SemaphoreType.DMA((2, 2))],
    )
    return pl.pallas_call(
        functools.partial(_attention_kernel, layer=layer, n_windows=pages_per_step,
                          pages_per_block=pages_per_block, n_blk_sample=n_blk_s, steps_per_seq=steps_per_seq),
        grid_spec=grid_spec,
        out_shape=[jax.ShapeDtypeStruct((b, s, D_ATTN), BF16), jax.ShapeDtypeStruct((db, t_new, D_ATTN), F32)],
        compiler_params=pltpu.CompilerParams(dimension_semantics=("arbitrary", "arbitrary", "arbitrary"),
                                             vmem_limit_bytes=VMEM_LIMIT_BYTES),
        name="moba_attention",
    )(page_table, rel_bias, q_t, k_blocks, vt_blocks, k_means, bias_tiles, q_rows, k_new, v_new, sbias,
      cache_kt, cache_vt)


def _merge_kernel(x_ref, attn_ref, ga_ref, u_ref, vn_ref, gb_ref, ma_ref, mb_ref,
                  mix_ref, sb_ref, wpa_ref, wpb_ref, wo_ref, fg_ref, o_ref, *, final):
    tm = x_ref.shape[0]
    d_sgu = u_ref.shape[-1]
    group_dim = d_sgu // N_GROUPS
    first_of_pair = lax.broadcasted_iota(jnp.int32, (CHUNK, 2 * group_dim), 1) < group_dim
    z_chunks = []
    for c in range(tm // CHUNK):
        vn = vn_ref[c * CHUNK:(c + 1) * CHUNK, :].astype(BF16)
        z_pairs = []
        for j in range(N_GROUPS // 2):
            v_pair = vn[:, j * 2 * group_dim:(j + 1) * 2 * group_dim]
            z_a = jnp.dot(mix_ref[2 * j], v_pair, preferred_element_type=F32)
            z_b = jnp.dot(mix_ref[2 * j + 1], v_pair, preferred_element_type=F32)
            z_pairs.append(jnp.where(first_of_pair, z_a, z_b))
        z_chunks.append(jnp.concatenate(z_pairs, axis=1) + sb_ref[...])
    z = jnp.concatenate(z_chunks, axis=0) if len(z_chunks) > 1 else z_chunks[0]
    y_b = (u_ref[...].astype(F32) * z) * _silu(gb_ref[...].astype(F32))
    y_a = attn_ref[...].astype(F32) * _silu(ga_ref[...].astype(F32))
    p_a = jnp.dot(y_a.astype(BF16), wpa_ref[...], preferred_element_type=F32)
    p_b = jnp.dot(y_b.astype(BF16), wpb_ref[...], preferred_element_type=F32)
    mixed = (jax.nn.sigmoid(ma_ref[...].astype(F32)) * p_a
             + jax.nn.sigmoid(mb_ref[...].astype(F32)) * p_b)
    out = x_ref[...] + jnp.dot(mixed.astype(BF16), wo_ref[...], preferred_element_type=F32)
    if final:
        out = out * lax.rsqrt(jnp.mean(out * out, axis=-1, keepdims=True) + EPS) * fg_ref[...]
    o_ref[...] = out


def _merge(x, attn, ga, u, vn, gb, ma, mb, mix_bf, sgu_bias, wpa_bf, wpb_bf, wo_bf, final_g, tm, final):
    n_tok, d_model = x.shape
    d_sgu = u.shape[-1]
    assert n_tok % tm == 0 and tm % CHUNK == 0
    row = lambda i: (i, 0)
    fixed2 = lambda i: (0, 0)
    narrow = pl.BlockSpec((tm, d_sgu), row)
    wide = pl.BlockSpec((tm, d_model), row)
    return pl.pallas_call(
        functools.partial(_merge_kernel, final=final),
        grid=(n_tok // tm,),
        in_specs=[wide, pl.BlockSpec((tm, D_ATTN), row), pl.BlockSpec((tm, D_ATTN), row),
                  narrow, narrow, narrow, wide, wide,
                  pl.BlockSpec(mix_bf.shape, lambda i: (0, 0, 0)),
                  pl.BlockSpec(sgu_bias.shape, fixed2),
                  pl.BlockSpec(wpa_bf.shape, fixed2),
                  pl.BlockSpec(wpb_bf.shape, fixed2),
                  pl.BlockSpec(wo_bf.shape, fixed2),
                  pl.BlockSpec((1, d_model), fixed2)],
        out_specs=wide,
        out_shape=jax.ShapeDtypeStruct((n_tok, d_model), F32),
        compiler_params=pltpu.CompilerParams(dimension_semantics=("arbitrary",),
                                             vmem_limit_bytes=VMEM_LIMIT_BYTES),
        name="merge",
    )(x, attn, ga, u, vn, gb, ma, mb, mix_bf, sgu_bias, wpa_bf, wpb_bf, wo_bf, final_g.reshape(1, d_model))


PROMPT_TOKEN_TILE = 256
MERGE_TOKEN_TILE = 512
SAMPLE_BLOCKS_PER_STEP = 8


def kernel(x_prompt, x_sample, cache_k, cache_v, page_table, rel_bias, norm_g, w_in, sgu_ln_g, sgu_ln_b,
           w_s, b_s, w_pa, w_pb, w_o, final_norm_g):
    b, s, d_model = x_prompt.shape
    db, t_new, _ = x_sample.shape
    depth = w_in.shape[0]
    d_sgu = sgu_ln_g.shape[-1]
    group_dim = d_sgu // N_GROUPS
    assert s % CHUNK == 0 and (db * t_new) % CHUNK == 0 and CHUNK % t_new == 0
    assert cache_k.shape[3] == N_HEADS and cache_k.shape[4] == HEAD_DIM

    bias_tiles = _prompt_bias_tiles(rel_bias)
    sbias = _sample_bias(rel_bias, t_new)
    cache_kt = jnp.transpose(cache_k, (0, 1, 3, 4, 2))
    cache_vt = jnp.transpose(cache_v, (0, 1, 3, 4, 2))

    xp = x_prompt.reshape(b * s, d_model)
    xs = x_sample.reshape(db * t_new, d_model)
    kv_stack = None
    k_s, v_s, sgu_rows = [], [], []
    for l in range(depth):
        final = l == depth - 1
        wqkv = w_in[l][:, :3 * D_ATTN]
        wqkv_bf, wqkv_t_bf = wqkv.astype(BF16), wqkv.T.astype(BF16)
        w_bf = w_in[l][:, 3 * D_ATTN:].astype(BF16)
        wpa_bf, wpb_bf, wo_bf = w_pa[l].astype(BF16), w_pb[l].astype(BF16), w_o[l].astype(BF16)
        mix_p = jnp.tril(w_s[l]).astype(BF16)
        bias_p = jnp.repeat(b_s[l].T, group_dim, axis=1)
        corner = jnp.tril(w_s[l][:, :t_new, :t_new])
        eye = jnp.eye(CHUNK // t_new, dtype=F32)
        mix_s = jnp.einsum("ab,gts->gatbs", eye, corner).reshape(N_GROUPS, CHUNK, CHUNK).astype(BF16)
        bias_s = jnp.tile(jnp.repeat(b_s[l][:, :t_new].T, group_dim, axis=1), (CHUNK // t_new, 1))

        q_t, k_all, v_all, ga, u, vn, gb, ma, mb, k_blocks, vt_blocks, k_means = _inproj(
            xp, norm_g[l], wqkv_t_bf, w_bf, sgu_ln_g[l], sgu_ln_b[l], PROMPT_TOKEN_TILE,
            stack_shape=(depth, b, D_ATTN, s), kv_stack=kv_stack, layer=l, seq_len=s)
        kv_stack = (k_all, v_all)
        q, k, v, ga_s, u_s, vn_s, gb_s, ma_s, mb_s = _inproj(
            xs, norm_g[l], wqkv_bf, w_bf, sgu_ln_g[l], sgu_ln_b[l], CHUNK)

        attn, attn_s = _attention(
            rel_bias, q_t, k_blocks, vt_blocks, k_means, bias_tiles,
            q.reshape(db, t_new, D_ATTN), k.reshape(db, t_new, D_ATTN), v.reshape(db, t_new, D_ATTN),
            cache_kt, cache_vt, l, page_table, sbias, SAMPLE_BLOCKS_PER_STEP)

        xp = _merge(xp, attn.reshape(b * s, D_ATTN), ga, u, vn, gb, ma, mb, mix_p, bias_p,
                    wpa_bf, wpb_bf, wo_bf, final_norm_g, MERGE_TOKEN_TILE, final)
        xs = _merge(xs, attn_s.reshape(db * t_new, D_ATTN), ga_s, u_s, vn_s, gb_s, ma_s, mb_s, mix_s, bias_s,
                    wpa_bf, wpb_bf, wo_bf, final_norm_g, CHUNK, final)
        k_s.append(k.reshape(db, t_new, N_HEADS, HEAD_DIM))
        v_s.append(v.reshape(db, t_new, N_HEADS, HEAD_DIM))
        sgu_rows.append(vn_s.reshape(db, t_new, d_sgu))

    k_prompt = jnp.transpose(k_all.reshape(depth, b, N_HEADS, HEAD_DIM, s), (0, 1, 4, 2, 3))
    v_prompt = jnp.transpose(v_all.reshape(depth, b, N_HEADS, HEAD_DIM, s), (0, 1, 4, 2, 3))
    return (xp.reshape(b, s, d_model), xs.reshape(db, t_new, d_model),
            k_prompt, v_prompt, jnp.stack(k_s), jnp.stack(v_s), jnp.stack(sgu_rows))
```
